```python
import jax, jax.numpy as jnp
from jax import lax
import numpy as np

D_MODEL = 1024
BATCH = 4
SEQ = 4096
DEPTH = 2

N_A_LAYERS = DEPTH // 2
N_B_LAYERS = DEPTH - N_A_LAYERS
HG_HEADS = 8
HG_DK = D_MODEL // HG_HEADS
HG_DV = D_MODEL // HG_HEADS
HG_CHUNK = 32
SW_Q_HEADS = 16
SW_KV_HEADS = 4
SW_HEAD_DIM = D_MODEL // SW_Q_HEADS
SW_GROUP = SW_Q_HEADS // SW_KV_HEADS
SW_WINDOW = 128
ROPE_THETA = 500000.0
ROPE_DIMS = SW_HEAD_DIM // 4
PEER_HEADS = 8
PEER_NKEYS = 128
PEER_EXPERTS = PEER_NKEYS * PEER_NKEYS
PEER_QDIM = 256
PEER_HALF = PEER_QDIM // 2
PEER_TOPK = 16
PEER_TOKEN_BLOCK = 128
DN_ALPHA = (2.0 * DEPTH) ** 0.25
DN_BETA = (8.0 * DEPTH) ** -0.25
LN_EPS = 1e-5
RMS_EPS = 1e-6

kernel_name = "yoco_hgrn2_swa_sink_peer_deepnorm"

F32 = jnp.float32


def _layernorm(x, g, b):
    xf = x.astype(F32)
    mu = jnp.mean(xf, -1, keepdims=True)
    var = jnp.mean(jnp.square(xf - mu), -1, keepdims=True)
    y = (xf - mu) * lax.rsqrt(var + LN_EPS) * g.astype(F32) + b.astype(F32)
    return y.astype(x.dtype)


def _rope_tables(positions):
    inv = ROPE_THETA ** (-jnp.arange(0, ROPE_DIMS, 2, dtype=F32) / ROPE_DIMS)
    ang = positions.astype(F32)[..., None] * inv
    return jnp.cos(ang), jnp.sin(ang)


def _partial_rope(x, cos, sin):
    half = ROPE_DIMS // 2
    xf = x.astype(F32)
    x1 = xf[..., :half]
    x2 = xf[..., half:ROPE_DIMS]
    c = cos[:, :, None, :]
    s = sin[:, :, None, :]
    out = jnp.concatenate([x1 * c - x2 * s, x2 * c + x1 * s, xf[..., ROPE_DIMS:]], axis=-1)
    return out.astype(x.dtype)


def _hgrn2(h, w_in, lb, norm_w, w_o):
    B, S, _ = h.shape
    H, DK, DV, C = HG_HEADS, HG_DK, HG_DV, HG_CHUNK
    nC = S // C
    proj = h @ w_in
    q, f, i, g = jnp.split(proj, [H * DK, 2 * H * DK, 2 * H * DK + H * DV], axis=-1)
    q = jax.nn.silu(q.astype(F32))
    lbf = lb.astype(F32)
    fg = lbf + (1.0 - lbf) * jax.nn.sigmoid(f.astype(F32))
    k = 1.0 - fg
    logf = jnp.log(fg)

    def chunk(t, d):
        return t.reshape(B, nC, C, H, d).transpose(1, 0, 3, 2, 4)

    qc, kc, lc = chunk(q, DK), chunk(k, DK), chunk(logf, DK)
    vc = chunk(i.astype(F32), DV)
    b = jnp.cumsum(lc, axis=3)
    b_last = b[:, :, :, -1:, :]
    qd = qc * jnp.exp(b)
    kd = kc * jnp.exp(-b)
    kl = kc * jnp.exp(b_last - b)
    causal = jnp.tril(jnp.ones((C, C), dtype=bool))
    att = jnp.where(causal, jnp.einsum('nbhid,nbhjd->nbhij', qd, kd), 0.0)
    o_intra = jnp.einsum('nbhij,nbhje->nbhie', att, vc)

    def step(state, xs):
        qd_n, kl_n, v_n, dec_n = xs
        o = jnp.einsum('bhid,bhde->bhie', qd_n, state)
        state = state * dec_n[..., None] + jnp.einsum('bhid,bhie->bhde', kl_n, v_n)
        return state, o

    s0 = jnp.zeros((B, H, DK, DV), F32)
    _, o_inter = lax.scan(step, s0, (qd, kl, vc, jnp.exp(b_last[:, :, :, 0, :])))
    o = (o_intra + o_inter).transpose(1, 0, 3, 2, 4).reshape(B, S, H, DV)
    o = o * lax.rsqrt(jnp.mean(jnp.square(o), -1, keepdims=True) + RMS_EPS)
    o = o.reshape(B, S, H * DV) * norm_w.astype(F32) * jax.nn.silu(g.astype(F32))
    return o.astype(h.dtype) @ w_o


def _shared_kv(h, w_kv, cos, sin):
    B, S, _ = h.shape
    kv = (h @ w_kv).reshape(B, S, 2, SW_KV_HEADS, SW_HEAD_DIM)
    k = _partial_rope(kv[:, :, 0], cos, sin)
    v = kv[:, :, 1]
    return k, v


def _swa_sink(h, k, v, w_q, sinks, w_o, cos, sin):
    B, S, _ = h.shape
    W, NKV, G, HD = SW_WINDOW, SW_KV_HEADS, SW_GROUP, SW_HEAD_DIM
    nb = S // W
    q = _partial_rope((h @ w_q).reshape(B, S, SW_Q_HEADS, HD), cos, sin)
    qb = q.reshape(B, nb, W, NKV, G, HD)

    def band(t):
        tp = jnp.pad(t, ((0, 0), (W, 0), (0, 0), (0, 0)))
        prev = tp[:, :S].reshape(B, nb, W, NKV, HD)
        cur = t.reshape(B, nb, W, NKV, HD)
        return jnp.concatenate([prev, cur], axis=2)

    kb, vb = band(k), band(v)
    logits = jnp.einsum('bnqhgd,bnkhd->bnhgqk', qb.astype(F32), kb.astype(F32)) * (HD ** -0.5)
    qi = jnp.arange(W)[:, None]
    kj = jnp.arange(2 * W)[None, :]
    diff = W + qi - kj
    in_win = (diff >= 0) & (diff < W)
    blk = jnp.arange(nb)[:, None, None]
    valid = in_win[None] & ((blk * W + kj[None] - W) >= 0)
    logits = jnp.where(valid[None, :, None, None], logits, -jnp.inf)
    sink = sinks.astype(F32).reshape(NKV, G)[None, None, :, :, None, None]
    full = jnp.concatenate([logits, jnp.broadcast_to(sink, logits.shape[:-1] + (1,))], axis=-1)
    p = jax.nn.softmax(full, axis=-1)[..., :-1]
    o = jnp.einsum('bnhgqk,bnkhd->bnqhgd', p, vb.astype(F32)).reshape(B, S, SW_Q_HEADS * HD)
    return o.astype(h.dtype) @ w_o


def _peer(h, w_pq, sub_keys, u_tab, v_tab):
    B, S, D = h.shape
    T = B * S
    K, NK, BLK = PEER_TOPK, PEER_NKEYS, PEER_TOKEN_BLOCK
    xf = h.reshape(T, D)
    q = (xf @ w_pq).astype(F32).reshape(T, PEER_HEADS, 2, PEER_HALF)
    s = jnp.einsum('thcd,hckd->thck', q, sub_keys.astype(F32))
    s1, i1 = lax.top_k(s[:, :, 0], K)
    s2, i2 = lax.top_k(s[:, :, 1], K)
    cand = (s1[..., :, None] + s2[..., None, :]).reshape(T, PEER_HEADS, K * K)
    cidx = (i1[..., :, None] * NK + i2[..., None, :]).reshape(T, PEER_HEADS, K * K)
    top_s, j = lax.top_k(cand, K)
    idx = jnp.take_along_axis(cidx, j, axis=-1)
    gate = jax.nn.softmax(top_s, axis=-1)
    nblk = T // BLK

    def block(args):
        xb, ib, gb = args
        a = jnp.einsum('td,thkd->thk', xb.astype(F32), u_tab[ib].astype(F32))
        w = gb * jax.nn.gelu(a, approximate=False)
        return jnp.einsum('thk,thkd->td', w, v_tab[ib].astype(F32))

    out = lax.map(block, (xf.reshape(nblk, BLK, D), idx.reshape(nblk, BLK, PEER_HEADS, K),
                          gate.reshape(nblk, BLK, PEER_HEADS, K)))
    return out.reshape(B, S, D).astype(h.dtype)


def setup_inputs(seed: int = 0) -> dict:
    key = jax.random.key(seed)
    ks = jax.random.split(key, 20)
    D = D_MODEL
    HGW = HG_HEADS * HG_DK
    nrm = jax.random.normal
    x = nrm(ks[0], (BATCH, SEQ, D), F32)
    positions = (jnp.arange(SEQ, dtype=jnp.int32)[None, :]
                 + jax.random.randint(ks[1], (BATCH, 1), 0, SEQ, dtype=jnp.int32))
    col_scale = jnp.concatenate([jnp.ones((2 * HGW,), F32), DN_BETA * jnp.ones((HGW,), F32),
                                 jnp.ones((HGW,), F32)]) * (D ** -0.5)
    hg_w_in = nrm(ks[2], (N_A_LAYERS, D, 4 * HGW), F32) * col_scale
    hg_lb = 0.1 * nrm(ks[3], (N_A_LAYERS + 1, HGW), F32)
    hg_norm_w = 1.0 + 0.02 * nrm(ks[4], (N_A_LAYERS, HG_HEADS * HG_DV), F32)
    hg_w_o = nrm(ks[5], (N_A_LAYERS, HG_HEADS * HG_DV, D), F32) * ((HG_HEADS * HG_DV) ** -0.5) * DN_BETA
    sw_w_q = nrm(ks[6], (N_B_LAYERS, D, SW_Q_HEADS * SW_HEAD_DIM), F32) * (D ** -0.5)
    sw_sinks = 0.5 * nrm(ks[7], (N_B_LAYERS, SW_Q_HEADS), F32)
    sw_w_o = nrm(ks[8], (N_B_LAYERS, SW_Q_HEADS * SW_HEAD_DIM, D), F32) * ((SW_Q_HEADS * SW_HEAD_DIM) ** -0.5) * DN_BETA
    kvw = SW_KV_HEADS * SW_HEAD_DIM
    w_kv = jnp.concatenate([nrm(ks[9], (D, kvw), F32) * (D ** -0.5),
                            nrm(ks[10], (D, kvw), F32) * (D ** -0.5) * DN_BETA], axis=1)
    peer_w_q = nrm(ks[11], (DEPTH, D, PEER_HEADS * PEER_QDIM), F32) * (D ** -0.5)
    peer_sub_keys = nrm(ks[12], (DEPTH, PEER_HEADS, 2, PEER_NKEYS, PEER_HALF), F32) * (PEER_HALF ** -0.5)
    peer_u = nrm(ks[13], (DEPTH, PEER_EXPERTS, D), F32) * (D ** -0.5)
    peer_v = nrm(ks[14], (DEPTH, PEER_EXPERTS, D), F32) * DN_BETA
    ln_mix_g = 1.0 + 0.02 * nrm(ks[15], (DEPTH, D), F32)
    ln_mix_b = 0.02 * nrm(ks[16], (DEPTH, D), F32)
    ln_ffn_g = 1.0 + 0.02 * nrm(ks[17], (DEPTH, D), F32)
    ln_ffn_b = 0.02 * nrm(ks[18], (DEPTH, D), F32)
    return {"x": x, "positions": positions, "hg_w_in": hg_w_in, "hg_lb": hg_lb,
            "hg_norm_w": hg_norm_w, "hg_w_o": hg_w_o, "sw_w_q": sw_w_q, "sw_sinks": sw_sinks,
            "sw_w_o": sw_w_o, "w_kv": w_kv, "peer_w_q": peer_w_q, "peer_sub_keys": peer_sub_keys,
            "peer_u": peer_u, "peer_v": peer_v, "ln_mix_g": ln_mix_g, "ln_mix_b": ln_mix_b,
            "ln_ffn_g": ln_ffn_g, "ln_ffn_b": ln_ffn_b}


def reference(x, positions, hg_w_in, hg_lb, hg_norm_w, hg_w_o, sw_w_q, sw_sinks, sw_w_o, w_kv,
              peer_w_q, peer_sub_keys, peer_u, peer_v, ln_mix_g, ln_mix_b, ln_ffn_g, ln_ffn_b):
    cos, sin = _rope_tables(positions)
    lb_all = jnp.cumsum(jax.nn.softmax(hg_lb.astype(F32), axis=0), axis=0)
    h = x
    k_sh = None
    v_sh = None
    for l in range(DEPTH):
        if l < N_A_LAYERS:
            mix = _hgrn2(h, hg_w_in[l], lb_all[l], hg_norm_w[l], hg_w_o[l])
        else:
            jb = l - N_A_LAYERS
            mix = _swa_sink(h, k_sh, v_sh, sw_w_q[jb], sw_sinks[jb], sw_w_o[jb], cos, sin)
        h = _layernorm(DN_ALPHA * h + mix, ln_mix_g[l], ln_mix_b[l])
        h = _layernorm(DN_ALPHA * h + _peer(h, peer_w_q[l], peer_sub_keys[l], peer_u[l], peer_v[l]),
                       ln_ffn_g[l], ln_ffn_b[l])
        if l == N_A_LAYERS - 1:
            k_sh, v_sh = _shared_kv(h, w_kv, cos, sin)
    return h
```

```python
import functools
import math

import jax
import jax.numpy as jnp
from jax import lax
from jax.experimental import pallas as pl
from jax.experimental.pallas import tpu as pltpu

F32 = jnp.float32
BF16 = jnp.bfloat16

D_MODEL = 1024
DEPTH = 2
HG_HEADS = 8
HG_DK = 128
HG_DV = 128
HG_CHUNK = 32
SW_Q_HEADS = 16
SW_KV_HEADS = 4
SW_HEAD_DIM = 64
SW_WINDOW = 128
ROPE_THETA = 500000.0
ROPE_DIMS = 16
PEER_HEADS = 8
PEER_NKEYS = 128
PEER_HALF = 128
PEER_TOPK = 16
DN_ALPHA = (2.0 * DEPTH) ** 0.25
LN_EPS = 1e-5
RMS_EPS = 1e-6

LANES = 128
VMEM_LIMIT = 56 * 1024 * 1024
NEG_INF = float("-inf")
INV_SQRT2 = 1.0 / math.sqrt(2.0)


def _sigmoid(x):
    return 1.0 / (1.0 + jnp.exp(-x))


def _nt_dot(a, b):
    return lax.dot_general(a, b, (((1,), (1,)), ((), ())), preferred_element_type=F32)


def _tn_dot(a, b):
    return lax.dot_general(a, b, (((0,), (0,)), ((), ())), preferred_element_type=F32)


def _hgrn2_kernel(x_ref, w_ref, lb_ref, nw_ref, o_ref, proj_ref, *, seq, chunk, row_tile):
    def project(i, carry):
        r = pl.ds(pl.multiple_of(i * row_tile, row_tile), row_tile)
        proj_ref[r, :] = jnp.dot(x_ref[0, r, :], w_ref[0], preferred_element_type=F32)
        return carry

    lax.fori_loop(0, seq // row_tile, project, 0)

    lb = lb_ref[0]
    nw = nw_ref[0]
    row = lax.broadcasted_iota(jnp.int32, (chunk, HG_DK), 0)
    causal = (lax.broadcasted_iota(jnp.int32, (chunk, chunk), 1)
              <= lax.broadcasted_iota(jnp.int32, (chunk, chunk), 0))

    def step(c, state_t):
        r = pl.ds(pl.multiple_of(c * chunk, chunk), chunk)
        q = proj_ref[r, 0:128]
        f = proj_ref[r, 128:256]
        v = proj_ref[r, 256:384]
        g = proj_ref[r, 384:512]
        q = q * _sigmoid(q)
        fg = lb + (1.0 - lb) * _sigmoid(f)
        k = 1.0 - fg
        b = jnp.log(fg)
        s = 1
        while s < chunk:
            b = b + jnp.where(row >= s, pltpu.roll(b, s, axis=0), 0.0)
            s *= 2
        b_last = b[chunk - 1:chunk, :]
        qd = (q * jnp.exp(b)).astype(BF16)
        kd = (k * jnp.exp(-b)).astype(BF16)
        kl = (k * jnp.exp(b_last - b)).astype(BF16)
        vb = v.astype(BF16)
        att = jnp.where(causal, _nt_dot(qd, kd), 0.0)
        o = (jnp.dot(att.astype(BF16), vb, preferred_element_type=F32)
             + _nt_dot(qd, state_t.astype(BF16)))
        state_t = state_t * jnp.exp(b_last) + _tn_dot(vb, kl)
        o = o * lax.rsqrt(jnp.mean(o * o, axis=-1, keepdims=True) + RMS_EPS)
        o_ref[0, r, :] = o * nw * (g * _sigmoid(g))
        return state_t

    lax.fori_loop(0, seq // chunk, step, jnp.zeros((HG_DV, HG_DK), F32), unroll=2)


def _hgrn2_mix(xb, w_heads, lb, norm_w):
    B, S, D = xb.shape
    H = w_heads.shape[0]
    kern = functools.partial(_hgrn2_kernel, seq=S, chunk=HG_CHUNK, row_tile=min(512, S))
    return pl.pallas_call(
        kern,
        grid=(B, H),
        in_specs=[
            pl.BlockSpec((1, S, D), lambda b, h: (b, 0, 0)),
            pl.BlockSpec((1, D, 4 * HG_DK), lambda b, h: (h, 0, 0)),
            pl.BlockSpec((1, 1, HG_DK), lambda b, h: (h, 0, 0)),
            pl.BlockSpec((1, 1, HG_DV), lambda b, h: (h, 0, 0)),
        ],
        out_specs=pl.BlockSpec((1, S, HG_DV), lambda b, h: (b, 0, h)),
        out_shape=jax.ShapeDtypeStruct((B, S, H * HG_DV), F32),
        scratch_shapes=[pltpu.VMEM((S, 4 * HG_DK), F32)],
        compiler_params=pltpu.CompilerParams(
            dimension_semantics=("parallel", "arbitrary"), vmem_limit_bytes=VMEM_LIMIT),
        name="hgrn2_mix",
    )(xb, w_heads, lb, norm_w)


def _layernorm_rows(z, g, b):
    mu = jnp.mean(z, axis=-1, keepdims=True)
    zc = z - mu
    var = jnp.mean(zc * zc, axis=-1, keepdims=True)
    return zc * lax.rsqrt(var + LN_EPS) * g + b


def _proj_res_ln_kernel(a_ref, w_ref, res_ref, g_ref, b_ref, o_ref):
    y = jnp.dot(a_ref[...].astype(BF16), w_ref[...], preferred_element_type=F32)
    o_ref[...] = _layernorm_rows(DN_ALPHA * res_ref[...] + y, g_ref[...], b_ref[...])


def _proj_res_ln(a, w_bf16, res, g, b, *, tm=512):
    T, K = a.shape
    N = w_bf16.shape[1]
    tm = min(tm, T)
    return pl.pallas_call(
        _proj_res_ln_kernel,
        grid=(T // tm,),
        in_specs=[
            pl.BlockSpec((tm, K), lambda i: (i, 0)),
            pl.BlockSpec((K, N), lambda i: (0, 0)),
            pl.BlockSpec((tm, N), lambda i: (i, 0)),
            pl.BlockSpec((1, N), lambda i: (0, 0)),
            pl.BlockSpec((1, N), lambda i: (0, 0)),
        ],
        out_specs=pl.BlockSpec((tm, N), lambda i: (i, 0)),
        out_shape=jax.ShapeDtypeStruct((T, N), F32),
        compiler_params=pltpu.CompilerParams(
            dimension_semantics=("parallel",), vmem_limit_bytes=VMEM_LIMIT),
        name="proj_res_ln",
    )(a, w_bf16, res, g.reshape(1, N), b.reshape(1, N))


def _top16_desc(s):
    vals = []
    rank = jnp.full(s.shape, 255.0, F32)
    for q in range(PEER_TOPK):
        m = jnp.max(s, axis=0, keepdims=True)
        hit = s == m
        rank = jnp.where(hit, float(q), rank)
        s = jnp.where(hit, NEG_INF, s)
        vals.append(m)
    return vals, rank


def _rows_to_slab(vals, lo):
    n = vals[0].shape[1]
    rid = lax.broadcasted_iota(jnp.int32, (8, n), 0)
    slab = jnp.zeros((8, n), F32)
    for r in range(8):
        slab = jnp.where(rid == r, vals[lo + r], slab)
    return slab


def _peer_route_kernel(ht_ref, wq_ref, sk_ref, ga_ref, cnt_ref, gb_ref, rk_ref, q_scr, s_scr, *, tm):
    hb = ht_ref[...].astype(BF16)
    q_scr[...] = jnp.dot(wq_ref[...], hb, preferred_element_type=F32).astype(BF16)
    for hc in range(2 * PEER_HEADS):
        s_scr[hc] = jnp.dot(sk_ref[hc], q_scr[hc * PEER_HALF:(hc + 1) * PEER_HALF, :],
                            preferred_element_type=F32)

    ncol = tm // LANES
    rid8 = lax.broadcasted_iota(jnp.int32, (8, LANES), 0)

    def body(it, carry):
        hh = it // ncol
        col = pl.ds(pl.multiple_of((it % ncol) * LANES, LANES), LANES)
        s1 = s_scr[2 * hh, :, col]
        s2 = s_scr[2 * hh + 1, :, col]
        a_vals, _ = _top16_desc(s1)
        b_vals, rank2 = _top16_desc(s2)
        a_lo = _rows_to_slab(a_vals, 0)
        a_hi = _rows_to_slab(a_vals, 8)
        slabs = [a_lo + b_vals[0], a_hi + b_vals[0]]
        for q in range(1, PEER_TOPK):
            keep = PEER_TOPK // (q + 1)
            slabs.append(jnp.where(rid8 < keep, a_lo + b_vals[q], NEG_INF))
        cand = jnp.concatenate(slabs, axis=0)
        for _ in range(PEER_TOPK - 1):
            m = jnp.max(cand, axis=0, keepdims=True)
            cand = jnp.where(cand == m, NEG_INF, cand)
        thr = jnp.max(cand, axis=0, keepdims=True)

        ea_lo = jnp.exp(a_lo - a_vals[0])
        ea_hi = jnp.exp(a_hi - a_vals[0])
        cnt = jnp.zeros(s1.shape, F32)
        zacc = jnp.zeros((8, LANES), F32)
        for q in range(PEER_TOPK):
            bq = b_vals[q]
            cnt = cnt + jnp.where(s1 + bq >= thr, 1.0, 0.0)
            part = (jnp.where(a_lo + bq >= thr, ea_lo, 0.0)
                    + jnp.where(a_hi + bq >= thr, ea_hi, 0.0))
            zacc = zacc + jnp.exp(bq - b_vals[0]) * part
        inv_z = 1.0 / jnp.sum(zacc, axis=0, keepdims=True)

        ga_ref[hh, :, col] = jnp.exp(s1 - a_vals[0])
        cnt_ref[hh, :, col] = cnt
        gb_ref[hh, :, col] = (jnp.exp(s2 - b_vals[0]) * inv_z).astype(BF16)
        rk_ref[hh, :, col] = rank2.astype(BF16)
        return carry

    lax.fori_loop(0, PEER_HEADS * ncol, body, 0)


def _peer_route(ht, wq_t, sub_keys, *, tm=512):
    D, T = ht.shape
    tm = min(tm, T)
    nq = wq_t.shape[0]
    kern = functools.partial(_peer_route_kernel, tm=tm)
    head_rows = pl.BlockSpec((PEER_HEADS, PEER_NKEYS, tm), lambda i: (0, 0, i))
    shp = (PEER_HEADS, PEER_NKEYS, T)
    return pl.pallas_call(
        kern,
        grid=(T // tm,),
        in_specs=[
            pl.BlockSpec((D, tm), lambda i: (0, i)),
            pl.BlockSpec((nq, D), lambda i: (0, 0)),
            pl.BlockSpec((2 * PEER_HEADS, PEER_NKEYS, PEER_HALF), lambda i: (0, 0, 0)),
        ],
        out_specs=[head_rows, head_rows, head_rows, head_rows],
        out_shape=[jax.ShapeDtypeStruct(shp, F32), jax.ShapeDtypeStruct(shp, F32),
                   jax.ShapeDtypeStruct(shp, BF16), jax.ShapeDtypeStruct(shp, BF16)],
        scratch_shapes=[pltpu.VMEM((nq, tm), BF16),
                        pltpu.VMEM((2 * PEER_HEADS, PEER_NKEYS, tm), F32)],
        compiler_params=pltpu.CompilerParams(
            dimension_semantics=("parallel",), vmem_limit_bytes=VMEM_LIMIT),
        name="peer_route",
    )(ht, wq_t, sub_keys)


def _peer_expert_kernel(ht_ref, u_ref, vt_ref, ga_ref, cnt_ref, gb_ref, rk_ref, g_ref, b_ref,
                        o_ref, xb_scr, a_scr, w_scr, *, tm, ni):
    e = pl.program_id(1)

    @pl.when(e == 0)
    def _():
        xb_scr[...] = ht_ref[...].astype(BF16)
        o_ref[...] = jnp.zeros(o_ref.shape, F32)

    a_scr[...] = jnp.dot(u_ref[...], xb_scr[...], preferred_element_type=F32)

    group = 4

    def cols(cc, carry):
        col = pl.ds(pl.multiple_of(cc * LANES, LANES), LANES)
        for g0 in range(0, ni, group):
            accs = [jnp.zeros((PEER_NKEYS, LANES), BF16) for _ in range(group)]
            for h in range(PEER_HEADS):
                rk = rk_ref[h, :, col]
                gb = gb_ref[h, :, col]
                for k in range(group):
                    il = g0 + k
                    n_row = cnt_ref[h, il:il + 1, col].astype(BF16)
                    a_row = ga_ref[h, il:il + 1, col].astype(BF16)
                    accs[k] = accs[k] + jnp.where(rk < n_row, gb * a_row, jnp.zeros_like(gb))
            for k in range(group):
                rows = slice((g0 + k) * PEER_NKEYS, (g0 + k + 1) * PEER_NKEYS)
                a = a_scr[rows, col]
                gelu = 0.5 * a * (1.0 + lax.erf(a * INV_SQRT2))
                w_scr[rows, col] = (accs[k].astype(F32) * gelu).astype(BF16)
        return carry

    lax.fori_loop(0, tm // LANES, cols, 0)
    o_ref[...] += jnp.dot(vt_ref[...], w_scr[...], preferred_element_type=F32)

    @pl.when(e == pl.num_programs(1) - 1)
    def _():
        def ln_cols(cc, carry):
            col = pl.ds(pl.multiple_of(cc * LANES, LANES), LANES)
            z = DN_ALPHA * ht_ref[:, col] + o_ref[:, col]
            mu = jnp.mean(z, axis=0, keepdims=True)
            zc = z - mu
            var = jnp.mean(zc * zc, axis=0, keepdims=True)
            o_ref[:, col] = zc * lax.rsqrt(var + LN_EPS) * g_ref[...] + b_ref[...]
            return carry

        lax.fori_loop(0, tm // LANES, ln_cols, 0)


def _peer_experts(ht, u_bf16, vt_bf16, ga, cnt, gb, rk, g, b, *, tm=1024, eb=1024):
    D, T = ht.shape
    E = u_bf16.shape[0]
    tm = min(tm, T)
    ni = eb // PEER_NKEYS
    kern = functools.partial(_peer_expert_kernel, tm=tm, ni=ni)
    return pl.pallas_call(
        kern,
        grid=(T // tm, E // eb),
        in_specs=[
            pl.BlockSpec((D, tm), lambda t, e: (0, t)),
            pl.BlockSpec((eb, D), lambda t, e: (e, 0)),
            pl.BlockSpec((D, eb), lambda t, e: (0, e)),
            pl.BlockSpec((PEER_HEADS, ni, tm), lambda t, e: (0, e, t)),
            pl.BlockSpec((PEER_HEADS, ni, tm), lambda t, e: (0, e, t)),
            pl.BlockSpec((PEER_HEADS, PEER_NKEYS, tm), lambda t, e: (0, 0, t)),
            pl.BlockSpec((PEER_HEADS, PEER_NKEYS, tm), lambda t, e: (0, 0, t)),
            pl.BlockSpec((D, 1), lambda t, e: (0, 0)),
            pl.BlockSpec((D, 1), lambda t, e: (0, 0)),
        ],
        out_specs=pl.BlockSpec((D, tm), lambda t, e: (0, t)),
        out_shape=jax.ShapeDtypeStruct((D, T), F32),
        scratch_shapes=[pltpu.VMEM((D, tm), BF16), pltpu.VMEM((eb, tm), F32),
                        pltpu.VMEM((eb, tm), BF16)],
        compiler_params=pltpu.CompilerParams(
            dimension_semantics=("parallel", "arbitrary"), vmem_limit_bytes=VMEM_LIMIT),
        name="peer_experts",
    )(ht, u_bf16, vt_bf16, ga, cnt, gb, rk, g.reshape(D, 1), b.reshape(D, 1))


def _peer_layer(h, w_pq, sub_keys, u_tab, v_tab, g, b):
    ht = h.T
    wq_t = w_pq.T.astype(BF16)
    sk = sub_keys.reshape(2 * PEER_HEADS, PEER_NKEYS, PEER_HALF).astype(BF16)
    ga, cnt, gb, rk = _peer_route(ht, wq_t, sk)
    out_t = _peer_experts(ht, u_tab.astype(BF16), v_tab.T.astype(BF16), ga, cnt, gb, rk, g, b)
    return out_t.T


def _rope_lane_tables():
    d = jnp.arange(LANES) % SW_HEAD_DIM
    half = ROPE_DIMS // 2
    inv = ROPE_THETA ** (-(2.0 * (d % half).astype(F32)) / ROPE_DIMS)
    inv = jnp.where(d < ROPE_DIMS, inv, 0.0)
    sgn_up = jnp.where(d < half, -1.0, 0.0)
    sgn_dn = jnp.where((d >= half) & (d < ROPE_DIMS), 1.0, 0.0)
    return jnp.stack([inv, sgn_up, sgn_dn]).astype(F32)


def _rope_apply(x, pos_f, tab):
    n, width = x.shape
    reps = width // LANES
    ang = pos_f * tab[0:1, :]
    cos = jnp.tile(jnp.cos(ang), (1, reps))
    sin = jnp.tile(jnp.sin(ang), (1, reps))
    s_up = jnp.tile(tab[1:2, :], (1, reps))
    s_dn = jnp.tile(tab[2:3, :], (1, reps))
    half = ROPE_DIMS // 2
    partner = pltpu.roll(x, width - half, axis=1) * s_up + pltpu.roll(x, half, axis=1) * s_dn
    return x * cos + partner * sin


def _kv_kernel(h_ref, w_ref, pos_ref, tab_ref, k_ref, v_ref):
    kv = jnp.dot(h_ref[...].astype(BF16), w_ref[...], preferred_element_type=F32)
    nk2 = SW_KV_HEADS * LANES
    k_ref[...] = _rope_apply(kv[:, :nk2], pos_ref[...].astype(F32), tab_ref[...]).astype(BF16)
    v_ref[...] = kv[:, nk2:].astype(BF16)


def _dup_heads(w):
    D, n = w.shape
    w3 = w.reshape(D, n // SW_HEAD_DIM, SW_HEAD_DIM)
    return jnp.concatenate([w3, w3], axis=2).reshape(D, 2 * n)


def _shared_kv(h, w_kv, pos, tab, *, tm=512):
    T, D = h.shape
    tm = min(tm, T)
    nk = SW_KV_HEADS * SW_HEAD_DIM
    nk2 = SW_KV_HEADS * LANES
    w_kv_bf16 = jnp.concatenate([_dup_heads(w_kv[:, :nk]), _dup_heads(w_kv[:, nk:])],
                                axis=1).astype(BF16)
    return pl.pallas_call(
        _kv_kernel,
        grid=(T // tm,),
        in_specs=[
            pl.BlockSpec((tm, D), lambda i: (i, 0)),
            pl.BlockSpec(w_kv_bf16.shape, lambda i: (0, 0)),
            pl.BlockSpec((tm, 1), lambda i: (i, 0)),
            pl.BlockSpec((3, LANES), lambda i: (0, 0)),
        ],
        out_specs=[pl.BlockSpec((tm, nk2), lambda i: (i, 0)),
                   pl.BlockSpec((tm, nk2), lambda i: (i, 0))],
        out_shape=[jax.ShapeDtypeStruct((T, nk2), BF16), jax.ShapeDtypeStruct((T, nk2), BF16)],
        compiler_params=pltpu.CompilerParams(
            dimension_semantics=("parallel",), vmem_limit_bytes=VMEM_LIMIT),
        name="shared_kv",
    )(h, w_kv_bf16, pos, tab)


def _swa_kernel(sink_ref, h_ref, wq_ref, k_ref, v_ref, pos_ref, tab_ref, wo_ref, g_ref, b_ref,
                o_ref, q_scr, att_scr, *, tq, seq):
    j = pl.program_id(1)
    W = SW_WINDOW
    hq = h_ref[...]
    q = jnp.dot(hq.astype(BF16), wq_ref[...], preferred_element_type=F32)
    q = _rope_apply(q, pos_ref[...].astype(F32), tab_ref[...]) * (SW_HEAD_DIM ** -0.5)
    q_scr[...] = q.astype(BF16)

    qi = lax.broadcasted_iota(jnp.int32, (W, 2 * W), 0)
    kj = lax.broadcasted_iota(jnp.int32, (W, 2 * W), 1)
    lane = lax.broadcasted_iota(jnp.int32, (W, LANES), 1)
    first_half = lane < SW_HEAD_DIM
    group = SW_Q_HEADS // SW_KV_HEADS

    def block(n, carry):
        q0 = j * tq + n * W
        k0 = jnp.maximum(q0 - W, 0)
        rows = pl.ds(pl.multiple_of(n * W, W), W)
        band = pl.ds(pl.multiple_of(k0, W), 2 * W)
        diff = (q0 + qi) - (k0 + kj)
        valid = (diff >= 0) & (diff < W)
        for pair in range(SW_Q_HEADS // 2):
            kvh = (2 * pair) // group
            lanes = slice(pair * LANES, (pair + 1) * LANES)
            qp = q_scr[rows, lanes]
            kb = k_ref[0, band, kvh * LANES:(kvh + 1) * LANES]
            vb = v_ref[0, band, kvh * LANES:(kvh + 1) * LANES]
            outs = []
            for sub in range(2):
                head = 2 * pair + sub
                keep = first_half if sub == 0 else jnp.logical_not(first_half)
                qh = jnp.where(keep, qp, jnp.zeros_like(qp))
                logits = jnp.where(valid, _nt_dot(qh, kb), NEG_INF)
                sink = sink_ref[head]
                m = jnp.maximum(jnp.max(logits, axis=-1, keepdims=True), sink)
                p = jnp.exp(logits - m)
                denom = jnp.sum(p, axis=-1, keepdims=True) + jnp.exp(sink - m)
                p = p * (1.0 / denom)
                outs.append(jnp.dot(p.astype(BF16), vb, preferred_element_type=F32))
            att_scr[rows, lanes] = jnp.where(first_half, outs[0], outs[1]).astype(BF16)
        return carry

    lax.fori_loop(0, tq // W, block, 0)
    y = jnp.dot(att_scr[...], wo_ref[...], preferred_element_type=F32)
    o_ref[...] = _layernorm_rows(DN_ALPHA * hq + y, g_ref[...], b_ref[...])


def _swa_layer(h, k_dup, v_dup, pos, tab, w_q_bf16, sinks, w_o_bf16, g, b, *, batch, tq=512):
    T, D = h.shape
    S = T // batch
    tq = min(tq, S)
    nq = S // tq
    nk2 = k_dup.shape[1]
    k3 = k_dup.reshape(batch, S, nk2)
    v3 = v_dup.reshape(batch, S, nk2)
    kern = functools.partial(_swa_kernel, tq=tq, seq=S)
    return pl.pallas_call(
        kern,
        grid=(batch, nq),
        in_specs=[
            pl.BlockSpec(memory_space=pltpu.SMEM),
            pl.BlockSpec((tq, D), lambda bb, j: (bb * nq + j, 0)),
            pl.BlockSpec((D, D), lambda bb, j: (0, 0)),
            pl.BlockSpec((1, S, nk2), lambda bb, j: (bb, 0, 0)),
            pl.BlockSpec((1, S, nk2), lambda bb, j: (bb, 0, 0)),
            pl.BlockSpec((tq, 1), lambda bb, j: (bb * nq + j, 0)),
            pl.BlockSpec((3, LANES), lambda bb, j: (0, 0)),
            pl.BlockSpec((D, D), lambda bb, j: (0, 0)),
            pl.BlockSpec((1, D), lambda bb, j: (0, 0)),
            pl.BlockSpec((1, D), lambda bb, j: (0, 0)),
        ],
        out_specs=pl.BlockSpec((tq, D), lambda bb, j: (bb * nq + j, 0)),
        out_shape=jax.ShapeDtypeStruct((T, D), F32),
        scratch_shapes=[pltpu.VMEM((tq, D), BF16), pltpu.VMEM((tq, D), BF16)],
        compiler_params=pltpu.CompilerParams(
            dimension_semantics=("parallel", "arbitrary"), vmem_limit_bytes=VMEM_LIMIT),
        name="swa_layer",
    )(sinks, h, w_q_bf16, k3, v3, pos, tab, w_o_bf16, g.reshape(1, D), b.reshape(1, D))


def kernel(x, positions, hg_w_in, hg_lb, hg_norm_w, hg_w_o, sw_w_q, sw_sinks, sw_w_o, w_kv,
           peer_w_q, peer_sub_keys, peer_u, peer_v, ln_mix_g, ln_mix_b, ln_ffn_g, ln_ffn_b):
    B, S, D = x.shape
    T = B * S
    H = HG_HEADS
    pos = positions.reshape(T, 1)
    tab = _rope_lane_tables()

    lb_all = jnp.cumsum(jax.nn.softmax(hg_lb.astype(F32), axis=0), axis=0)
    w_heads = (hg_w_in[0].reshape(D, 4, H, HG_DK).transpose(2, 0, 1, 3)
               .reshape(H, D, 4 * HG_DK).astype(BF16))
    mix = _hgrn2_mix(x.astype(BF16), w_heads, lb_all[0].reshape(H, 1, HG_DK),
                     hg_norm_w[0].reshape(H, 1, HG_DV))
    h = _proj_res_ln(mix.reshape(T, H * HG_DV), hg_w_o[0].astype(BF16), x.reshape(T, D),
                     ln_mix_g[0], ln_mix_b[0])
    h = _peer_layer(h, peer_w_q[0], peer_sub_keys[0], peer_u[0], peer_v[0],
                    ln_ffn_g[0], ln_ffn_b[0])

    k_dup, v_dup = _shared_kv(h, w_kv, pos, tab)
    h = _swa_layer(h, k_dup, v_dup, pos, tab, sw_w_q[0].astype(BF16), sw_sinks[0].astype(F32),
                   sw_w_o[0].astype(BF16), ln_mix_g[1], ln_mix_b[1], batch=B)
    h = _peer_layer(h, peer_w_q[1], peer_sub_keys[1], peer_u[1], peer_v[1],
                    ln_ffn_g[1], ln_ffn_b[1])
    return h.reshape(B, S, D)
```

```python
import functools
import math

import jax
import jax.numpy as jnp
from jax import lax
from jax.experimental import pallas as pl
from jax.experimental.pallas import tpu as pltpu

F32 = jnp.float32
BF16 = jnp.bfloat16

D_MODEL = 1024
DEPTH = 2
HG_HEADS = 8
HG_DK = 128
HG_DV = 128
HG_CHUNK = 32
SW_Q_HEADS = 16
SW_KV_HEADS = 4
SW_HEAD_DIM = 64
SW_WINDOW = 128
ROPE_THETA = 500000.0
ROPE_DIMS = 16
PEER_HEADS = 8
PEER_NKEYS = 128
PEER_HALF = 128
PEER_TOPK = 16
DN_ALPHA = (2.0 * DEPTH) ** 0.25
LN_EPS = 1e-5
RMS_EPS = 1e-6

LANES = 128
VMEM_LIMIT = 56 * 1024 * 1024
NEG_INF = float("-inf")
INV_SQRT2 = 1.0 / math.sqrt(2.0)


def _sigmoid(x):
    return 1.0 / (1.0 + jnp.exp(-x))


def _nt_dot(a, b):
    return lax.dot_general(a, b, (((1,), (1,)), ((), ())), preferred_element_type=F32)


def _tn_dot(a, b):
    return lax.dot_general(a, b, (((0,), (0,)), ((), ())), preferred_element_type=F32)


def _hgrn2_kernel(x_ref, w_ref, lb_ref, nw_ref, o_ref, proj_ref, *, seq, chunk, row_tile):
    def project(i, carry):
        r = pl.ds(pl.multiple_of(i * row_tile, row_tile), row_tile)
        proj_ref[r, :] = jnp.dot(x_ref[0, r, :], w_ref[0], preferred_element_type=F32)
        return carry

    lax.fori_loop(0, seq // row_tile, project, 0)

    lb = lb_ref[0]
    nw = nw_ref[0]
    row = lax.broadcasted_iota(jnp.int32, (chunk, HG_DK), 0)
    causal = (lax.broadcasted_iota(jnp.int32, (chunk, chunk), 1)
              <= lax.broadcasted_iota(jnp.int32, (chunk, chunk), 0))

    def step(c, state_t):
        r = pl.ds(pl.multiple_of(c * chunk, chunk), chunk)
        q = proj_ref[r, 0:128]
        f = proj_ref[r, 128:256]
        v = proj_ref[r, 256:384]
        g = proj_ref[r, 384:512]
        q = q * _sigmoid(q)
        fg = lb + (1.0 - lb) * _sigmoid(f)
        k = 1.0 - fg
        b = jnp.log(fg)
        s = 1
        while s < chunk:
            b = b + jnp.where(row >= s, pltpu.roll(b, s, axis=0), 0.0)
            s *= 2
        b_last = b[chunk - 1:chunk, :]
        qd = (q * jnp.exp(b)).astype(BF16)
        kd = (k * jnp.exp(-b)).astype(BF16)
        kl = (k * jnp.exp(b_last - b)).astype(BF16)
        vb = v.astype(BF16)
        att = jnp.where(causal, _nt_dot(qd, kd), 0.0)
        o = (jnp.dot(att.astype(BF16), vb, preferred_element_type=F32)
             + _nt_dot(qd, state_t.astype(BF16)))
        state_t = state_t * jnp.exp(b_last) + _tn_dot(vb, kl)
        o = o * lax.rsqrt(jnp.mean(o * o, axis=-1, keepdims=True) + RMS_EPS)
        o_ref[0, r, :] = o * nw * (g * _sigmoid(g))
        return state_t

    lax.fori_loop(0, seq // chunk, step, jnp.zeros((HG_DV, HG_DK), F32), unroll=2)


def _hgrn2_mix(xb, w_heads, lb, norm_w):
    B, S, D = xb.shape
    H = w_heads.shape[0]
    kern = functools.partial(_hgrn2_kernel, seq=S, chunk=HG_CHUNK, row_tile=min(512, S))
    return pl.pallas_call(
        kern,
        grid=(B, H),
        in_specs=[
            pl.BlockSpec((1, S, D), lambda b, h: (b, 0, 0)),
            pl.BlockSpec((1, D, 4 * HG_DK), lambda b, h: (h, 0, 0)),
            pl.BlockSpec((1, 1, HG_DK), lambda b, h: (h, 0, 0)),
            pl.BlockSpec((1, 1, HG_DV), lambda b, h: (h, 0, 0)),
        ],
        out_specs=pl.BlockSpec((1, S, HG_DV), lambda b, h: (b, 0, h)),
        out_shape=jax.ShapeDtypeStruct((B, S, H * HG_DV), F32),
        scratch_shapes=[pltpu.VMEM((S, 4 * HG_DK), F32)],
        compiler_params=pltpu.CompilerParams(
            dimension_semantics=("parallel", "arbitrary"), vmem_limit_bytes=VMEM_LIMIT),
        name="hgrn2_mix",
    )(xb, w_heads, lb, norm_w)


def _layernorm_rows(z, g, b):
    mu = jnp.mean(z, axis=-1, keepdims=True)
    zc = z - mu
    var = jnp.mean(zc * zc, axis=-1, keepdims=True)
    return zc * lax.rsqrt(var + LN_EPS) * g + b


def _proj_res_ln_kernel(a_ref, w_ref, res_ref, g_ref, b_ref, o_ref):
    y = jnp.dot(a_ref[...].astype(BF16), w_ref[...], preferred_element_type=F32)
    o_ref[...] = _layernorm_rows(DN_ALPHA * res_ref[...] + y, g_ref[...], b_ref[...])


def _proj_res_ln(a, w_bf16, res, g, b, *, tm=512):
    T, K = a.shape
    N = w_bf16.shape[1]
    tm = min(tm, T)
    return pl.pallas_call(
        _proj_res_ln_kernel,
        grid=(T // tm,),
        in_specs=[
            pl.BlockSpec((tm, K), lambda i: (i, 0)),
            pl.BlockSpec((K, N), lambda i: (0, 0)),
            pl.BlockSpec((tm, N), lambda i: (i, 0)),
            pl.BlockSpec((1, N), lambda i: (0, 0)),
            pl.BlockSpec((1, N), lambda i: (0, 0)),
        ],
        out_specs=pl.BlockSpec((tm, N), lambda i: (i, 0)),
        out_shape=jax.ShapeDtypeStruct((T, N), F32),
        compiler_params=pltpu.CompilerParams(
            dimension_semantics=("parallel",), vmem_limit_bytes=VMEM_LIMIT),
        name="proj_res_ln",
    )(a, w_bf16, res, g.reshape(1, N), b.reshape(1, N))


def _top16_desc(s):
    vals = []
    rank = jnp.full(s.shape, 255.0, F32)
    for q in range(PEER_TOPK):
        m = jnp.max(s, axis=0, keepdims=True)
        hit = s == m
        rank = jnp.where(hit, float(q), rank)
        s = jnp.where(hit, NEG_INF, s)
        vals.append(m)
    return vals, rank


def _rows_to_slab(vals, lo):
    n = vals[0].shape[1]
    rid = lax.broadcasted_iota(jnp.int32, (8, n), 0)
    slab = jnp.zeros((8, n), F32)
    for r in range(8):
        slab = jnp.where(rid == r, vals[lo + r], slab)
    return slab


def _peer_route_kernel(ht_ref, wq_ref, sk_ref, ga_ref, cnt_ref, gb_ref, rk_ref, q_scr, s_scr, *, tm):
    hb = ht_ref[...].astype(BF16)
    q_scr[...] = jnp.dot(wq_ref[...], hb, preferred_element_type=F32).astype(BF16)
    for hc in range(2 * PEER_HEADS):
        s_scr[hc] = jnp.dot(sk_ref[hc], q_scr[hc * PEER_HALF:(hc + 1) * PEER_HALF, :],
                            preferred_element_type=F32)

    ncol = tm // LANES
    rid8 = lax.broadcasted_iota(jnp.int32, (8, LANES), 0)

    def body(it, carry):
        hh = it // ncol
        col = pl.ds(pl.multiple_of((it % ncol) * LANES, LANES), LANES)
        s1 = s_scr[2 * hh, :, col]
        s2 = s_scr[2 * hh + 1, :, col]
        a_vals, _ = _top16_desc(s1)
        b_vals, rank2 = _top16_desc(s2)
        a_lo = _rows_to_slab(a_vals, 0)
        a_hi = _rows_to_slab(a_vals, 8)
        slabs = [a_lo + b_vals[0], a_hi + b_vals[0]]
        for q in range(1, PEER_TOPK):
            keep = PEER_TOPK // (q + 1)
            slabs.append(jnp.where(rid8 < keep, a_lo + b_vals[q], NEG_INF))
        cand = jnp.concatenate(slabs, axis=0)
        for _ in range(PEER_TOPK - 1):
            m = jnp.max(cand, axis=0, keepdims=True)
            cand = jnp.where(cand == m, NEG_INF, cand)
        thr = jnp.max(cand, axis=0, keepdims=True)

        ea_lo = jnp.exp(a_lo - a_vals[0])
        ea_hi = jnp.exp(a_hi - a_vals[0])
        cnt = jnp.zeros(s1.shape, F32)
        zacc = jnp.zeros((8, LANES), F32)
        for q in range(PEER_TOPK):
            bq = b_vals[q]
            cnt = cnt + jnp.where(s1 + bq >= thr, 1.0, 0.0)
            part = (jnp.where(a_lo + bq >= thr, ea_lo, 0.0)
                    + jnp.where(a_hi + bq >= thr, ea_hi, 0.0))
            zacc = zacc + jnp.exp(bq - b_vals[0]) * part
        inv_z = 1.0 / jnp.sum(zacc, axis=0, keepdims=True)

        ga_ref[hh, :, col] = jnp.exp(s1 - a_vals[0])
        cnt_ref[hh, :, col] = cnt
        gb_ref[hh, :, col] = jnp.exp(s2 - b_vals[0]) * inv_z
        rk_ref[hh, :, col] = rank2
        return carry

    lax.fori_loop(0, PEER_HEADS * ncol, body, 0)


def _peer_route(ht, wq_t, sub_keys, *, tm=512):
    D, T = ht.shape
    tm = min(tm, T)
    nq = wq_t.shape[0]
    kern = functools.partial(_peer_route_kernel, tm=tm)
    head_rows = pl.BlockSpec((PEER_HEADS, PEER_NKEYS, tm), lambda i: (0, 0, i))
    shp = (PEER_HEADS, PEER_NKEYS, T)
    return pl.pallas_call(
        kern,
        grid=(T // tm,),
        in_specs=[
            pl.BlockSpec((D, tm), lambda i: (0, i)),
            pl.BlockSpec((nq, D), lambda i: (0, 0)),
            pl.BlockSpec((2 * PEER_HEADS, PEER_NKEYS, PEER_HALF), lambda i: (0, 0, 0)),
        ],
        out_specs=[head_rows, head_rows, head_rows, head_rows],
        out_shape=[jax.ShapeDtypeStruct(shp, F32), jax.ShapeDtypeStruct(shp, F32),
                   jax.ShapeDtypeStruct(shp, F32), jax.ShapeDtypeStruct(shp, F32)],
        scratch_shapes=[pltpu.VMEM((nq, tm), BF16),
                        pltpu.VMEM((2 * PEER_HEADS, PEER_NKEYS, tm), F32)],
        compiler_params=pltpu.CompilerParams(
            dimension_semantics=("parallel",), vmem_limit_bytes=VMEM_LIMIT),
        name="peer_route",
    )(ht, wq_t, sub_keys)


def _peer_expert_kernel(xb_ref, u_ref, vt_ref, ga_ref, cnt_ref, gb_ref, rk_ref, res_ref, g_ref,
                        b_ref, o_ref, a_scr, w_scr, rk16_scr, gb16_scr, *, tm, ni):
    e = pl.program_id(1)

    @pl.when(e == 0)
    def _():
        o_ref[...] = jnp.zeros(o_ref.shape, F32)

        def pack_head(h, carry):
            rk16_scr[h] = rk_ref[h].astype(BF16)
            gb16_scr[h] = gb_ref[h].astype(BF16)
            return carry

        lax.fori_loop(0, PEER_HEADS, pack_head, 0)

    group = 4
    slab = 16
    nslab = PEER_NKEYS // slab
    wide = 2 * LANES

    def routing_weight(a_rd, w_wr, col):
        for g0 in range(0, ni, group):
            accs = [[jnp.zeros((slab, LANES), BF16) for _ in range(nslab)] for _ in range(group)]
            for h in range(PEER_HEADS):
                rk = rk16_scr[h, :, col]
                gb = gb16_scr[h, :, col]
                rks = [rk[i * slab:(i + 1) * slab] for i in range(nslab)]
                gbs = [gb[i * slab:(i + 1) * slab] for i in range(nslab)]
                for k in range(group):
                    il = g0 + k
                    n16 = jnp.broadcast_to(cnt_ref[h, il:il + 1, col], (slab, LANES)).astype(BF16)
                    a16 = jnp.broadcast_to(ga_ref[h, il:il + 1, col], (slab, LANES)).astype(BF16)
                    for i in range(nslab):
                        accs[k][i] = accs[k][i] + jnp.where(rks[i] < n16, gbs[i] * a16,
                                                            jnp.zeros_like(a16))
            for k in range(group):
                r0 = (g0 + k) * PEER_NKEYS
                a = a_rd[r0:r0 + PEER_NKEYS, col].astype(BF16)
                gelu = (0.5 * a) * (1.0 + lax.erf(a * INV_SQRT2))
                w_wr[r0:r0 + PEER_NKEYS, col] = jnp.concatenate(accs[k], axis=0) * gelu

    for c in range(tm // wide):
        cols = slice(c * wide, (c + 1) * wide)
        a_scr[:, cols] = jnp.dot(u_ref[...], xb_ref[:, cols], preferred_element_type=F32)
        for half in range(wide // LANES):
            lo = c * wide + half * LANES
            routing_weight(a_scr, w_scr, slice(lo, lo + LANES))
        o_ref[:, cols] += jnp.dot(vt_ref[...], w_scr[:, cols], preferred_element_type=F32)

    @pl.when(e == pl.num_programs(1) - 1)
    def _():
        def ln_cols(cc, carry):
            col = pl.ds(pl.multiple_of(cc * LANES, LANES), LANES)
            z = DN_ALPHA * res_ref[:, col] + o_ref[:, col]
            mu = jnp.mean(z, axis=0, keepdims=True)
            zc = z - mu
            var = jnp.mean(zc * zc, axis=0, keepdims=True)
            o_ref[:, col] = zc * lax.rsqrt(var + LN_EPS) * g_ref[...] + b_ref[...]
            return carry

        lax.fori_loop(0, tm // LANES, ln_cols, 0)


def _peer_experts(ht, u_bf16, vt_bf16, ga, cnt, gb, rk, g, b, *, tm=1024, eb=1024):
    D, T = ht.shape
    E = u_bf16.shape[0]
    tm = min(tm, T)
    ni = eb // PEER_NKEYS
    kern = functools.partial(_peer_expert_kernel, tm=tm, ni=ni)
    once = pl.Buffered(1)
    return pl.pallas_call(
        kern,
        grid=(T // tm, E // eb),
        in_specs=[
            pl.BlockSpec((D, tm), lambda t, e: (0, t), pipeline_mode=once),
            pl.BlockSpec((eb, D), lambda t, e: (e, 0)),
            pl.BlockSpec((D, eb), lambda t, e: (0, e)),
            pl.BlockSpec((PEER_HEADS, ni, tm), lambda t, e: (0, e, t)),
            pl.BlockSpec((PEER_HEADS, ni, tm), lambda t, e: (0, e, t)),
            pl.BlockSpec((PEER_HEADS, PEER_NKEYS, tm), lambda t, e: (0, 0, t), pipeline_mode=once),
            pl.BlockSpec((PEER_HEADS, PEER_NKEYS, tm), lambda t, e: (0, 0, t), pipeline_mode=once),
            pl.BlockSpec((D, tm), lambda t, e: (0, t), pipeline_mode=once),
            pl.BlockSpec((D, 1), lambda t, e: (0, 0)),
            pl.BlockSpec((D, 1), lambda t, e: (0, 0)),
        ],
        out_specs=pl.BlockSpec((D, tm), lambda t, e: (0, t)),
        out_shape=jax.ShapeDtypeStruct((D, T), F32),
        scratch_shapes=[pltpu.VMEM((eb, tm), F32), pltpu.VMEM((eb, tm), BF16),
                        pltpu.VMEM((PEER_HEADS, PEER_NKEYS, tm), BF16),
                        pltpu.VMEM((PEER_HEADS, PEER_NKEYS, tm), BF16)],
        compiler_params=pltpu.CompilerParams(
            dimension_semantics=("parallel", "arbitrary"), vmem_limit_bytes=VMEM_LIMIT),
        name="peer_experts",
    )(ht.astype(BF16), u_bf16, vt_bf16, ga, cnt, gb, rk, ht, g.reshape(D, 1), b.reshape(D, 1))


def _peer_layer(h, w_pq, sub_keys, u_tab, v_tab, g, b):
    ht = h.T
    wq_t = w_pq.T.astype(BF16)
    sk = sub_keys.reshape(2 * PEER_HEADS, PEER_NKEYS, PEER_HALF).astype(BF16)
    ga, cnt, gb, rk = _peer_route(ht, wq_t, sk)
    out_t = _peer_experts(ht, u_tab.astype(BF16), v_tab.T.astype(BF16), ga, cnt, gb, rk, g, b)
    return out_t.T


def _rope_lane_tables():
    d = jnp.arange(LANES) % SW_HEAD_DIM
    half = ROPE_DIMS // 2
    inv = ROPE_THETA ** (-(2.0 * (d % half).astype(F32)) / ROPE_DIMS)
    inv = jnp.where(d < ROPE_DIMS, inv, 0.0)
    sgn_up = jnp.where(d < half, -1.0, 0.0)
    sgn_dn = jnp.where((d >= half) & (d < ROPE_DIMS), 1.0, 0.0)
    return jnp.stack([inv, sgn_up, sgn_dn]).astype(F32)


def _rope_apply(x, pos_f, tab):
    n, width = x.shape
    reps = width // LANES
    ang = pos_f * tab[0:1, :]
    cos = jnp.tile(jnp.cos(ang), (1, reps))
    sin = jnp.tile(jnp.sin(ang), (1, reps))
    s_up = jnp.tile(tab[1:2, :], (1, reps))
    s_dn = jnp.tile(tab[2:3, :], (1, reps))
    half = ROPE_DIMS // 2
    partner = pltpu.roll(x, width - half, axis=1) * s_up + pltpu.roll(x, half, axis=1) * s_dn
    return x * cos + partner * sin


def _kv_kernel(h_ref, w_ref, pos_ref, tab_ref, k_ref, v_ref):
    kv = jnp.dot(h_ref[...].astype(BF16), w_ref[...], preferred_element_type=F32)
    nk2 = SW_KV_HEADS * LANES
    k_ref[...] = _rope_apply(kv[:, :nk2], pos_ref[...].astype(F32), tab_ref[...]).astype(BF16)
    v_ref[...] = kv[:, nk2:].astype(BF16)


def _dup_heads(w):
    D, n = w.shape
    w3 = w.reshape(D, n // SW_HEAD_DIM, SW_HEAD_DIM)
    return jnp.concatenate([w3, w3], axis=2).reshape(D, 2 * n)


def _shared_kv(h, w_kv, pos, tab, *, tm=512):
    T, D = h.shape
    tm = min(tm, T)
    nk = SW_KV_HEADS * SW_HEAD_DIM
    nk2 = SW_KV_HEADS * LANES
    w_kv_bf16 = jnp.concatenate([_dup_heads(w_kv[:, :nk]), _dup_heads(w_kv[:, nk:])],
                                axis=1).astype(BF16)
    return pl.pallas_call(
        _kv_kernel,
        grid=(T // tm,),
        in_specs=[
            pl.BlockSpec((tm, D), lambda i: (i, 0)),
            pl.BlockSpec(w_kv_bf16.shape, lambda i: (0, 0)),
            pl.BlockSpec((tm, 1), lambda i: (i, 0)),
            pl.BlockSpec((3, LANES), lambda i: (0, 0)),
        ],
        out_specs=[pl.BlockSpec((tm, nk2), lambda i: (i, 0)),
                   pl.BlockSpec((tm, nk2), lambda i: (i, 0))],
        out_shape=[jax.ShapeDtypeStruct((T, nk2), BF16), jax.ShapeDtypeStruct((T, nk2), BF16)],
        compiler_params=pltpu.CompilerParams(
            dimension_semantics=("parallel",), vmem_limit_bytes=VMEM_LIMIT),
        name="shared_kv",
    )(h, w_kv_bf16, pos, tab)


def _swa_kernel(sink_ref, h_ref, wq_ref, k_ref, v_ref, pos_ref, tab_ref, wo_ref, g_ref, b_ref,
                o_ref, q_scr, att_scr, *, tq, seq):
    j = pl.program_id(1)
    W = SW_WINDOW
    hq = h_ref[...]
    q = jnp.dot(hq.astype(BF16), wq_ref[...], preferred_element_type=F32)
    q = _rope_apply(q, pos_ref[...].astype(F32), tab_ref[...]) * (SW_HEAD_DIM ** -0.5)
    q_scr[...] = q.astype(BF16)

    qi = lax.broadcasted_iota(jnp.int32, (W, 2 * W), 0)
    kj = lax.broadcasted_iota(jnp.int32, (W, 2 * W), 1)
    lane = lax.broadcasted_iota(jnp.int32, (W, LANES), 1)
    first_half = lane < SW_HEAD_DIM
    group = SW_Q_HEADS // SW_KV_HEADS

    def block(n, carry):
        q0 = j * tq + n * W
        k0 = jnp.maximum(q0 - W, 0)
        rows = pl.ds(pl.multiple_of(n * W, W), W)
        band = pl.ds(pl.multiple_of(k0, W), 2 * W)
        diff = (q0 + qi) - (k0 + kj)
        valid = (diff >= 0) & (diff < W)
        for pair in range(SW_Q_HEADS // 2):
            kvh = (2 * pair) // group
            lanes = slice(pair * LANES, (pair + 1) * LANES)
            qp = q_scr[rows, lanes]
            kb = k_ref[0, band, kvh * LANES:(kvh + 1) * LANES]
            vb = v_ref[0, band, kvh * LANES:(kvh + 1) * LANES]
            outs = []
            for sub in range(2):
                head = 2 * pair + sub
                keep = first_half if sub == 0 else jnp.logical_not(first_half)
                qh = jnp.where(keep, qp, jnp.zeros_like(qp))
                logits = jnp.where(valid, _nt_dot(qh, kb), NEG_INF)
                sink = sink_ref[head]
                m = jnp.maximum(jnp.max(logits, axis=-1, keepdims=True), sink)
                p = jnp.exp(logits - m)
                denom = jnp.sum(p, axis=-1, keepdims=True) + jnp.exp(sink - m)
                p = p * (1.0 / denom)
                outs.append(jnp.dot(p.astype(BF16), vb, preferred_element_type=F32))
            att_scr[rows, lanes] = jnp.where(first_half, outs[0], outs[1]).astype(BF16)
        return carry

    lax.fori_loop(0, tq // W, block, 0)
    y = jnp.dot(att_scr[...], wo_ref[...], preferred_element_type=F32)
    o_ref[...] = _layernorm_rows(DN_ALPHA * hq + y, g_ref[...], b_ref[...])


def _swa_layer(h, k_dup, v_dup, pos, tab, w_q_bf16, sinks, w_o_bf16, g, b, *, batch, tq=512):
    T, D = h.shape
    S = T // batch
    tq = min(tq, S)
    nq = S // tq
    nk2 = k_dup.shape[1]
    k3 = k_dup.reshape(batch, S, nk2)
    v3 = v_dup.reshape(batch, S, nk2)
    kern = functools.partial(_swa_kernel, tq=tq, seq=S)
    return pl.pallas_call(
        kern,
        grid=(batch, nq),
        in_specs=[
            pl.BlockSpec(memory_space=pltpu.SMEM),
            pl.BlockSpec((tq, D), lambda bb, j: (bb * nq + j, 0)),
            pl.BlockSpec((D, D), lambda bb, j: (0, 0)),
            pl.BlockSpec((1, S, nk2), lambda bb, j: (bb, 0, 0)),
            pl.BlockSpec((1, S, nk2), lambda bb, j: (bb, 0, 0)),
            pl.BlockSpec((tq, 1), lambda bb, j: (bb * nq + j, 0)),
            pl.BlockSpec((3, LANES), lambda bb, j: (0, 0)),
            pl.BlockSpec((D, D), lambda bb, j: (0, 0)),
            pl.BlockSpec((1, D), lambda bb, j: (0, 0)),
            pl.BlockSpec((1, D), lambda bb, j: (0, 0)),
        ],
        out_specs=pl.BlockSpec((tq, D), lambda bb, j: (bb * nq + j, 0)),
        out_shape=jax.ShapeDtypeStruct((T, D), F32),
        scratch_shapes=[pltpu.VMEM((tq, D), BF16), pltpu.VMEM((tq, D), BF16)],
        compiler_params=pltpu.CompilerParams(
            dimension_semantics=("parallel", "arbitrary"), vmem_limit_bytes=VMEM_LIMIT),
        name="swa_layer",
    )(sinks, h, w_q_bf16, k3, v3, pos, tab, w_o_bf16, g.reshape(1, D), b.reshape(1, D))


def kernel(x, positions, hg_w_in, hg_lb, hg_norm_w, hg_w_o, sw_w_q, sw_sinks, sw_w_o, w_kv,
           peer_w_q, peer_sub_keys, peer_u, peer_v, ln_mix_g, ln_mix_b, ln_ffn_g, ln_ffn_b):
    B, S, D = x.shape
    T = B * S
    H = HG_HEADS
    pos = positions.reshape(T, 1)
    tab = _rope_lane_tables()

    lb_all = jnp.cumsum(jax.nn.softmax(hg_lb.astype(F32), axis=0), axis=0)
    w_heads = (hg_w_in[0].reshape(D, 4, H, HG_DK).transpose(2, 0, 1, 3)
               .reshape(H, D, 4 * HG_DK).astype(BF16))
    mix = _hgrn2_mix(x.astype(BF16), w_heads, lb_all[0].reshape(H, 1, HG_DK),
                     hg_norm_w[0].reshape(H, 1, HG_DV))
    h = _proj_res_ln(mix.reshape(T, H * HG_DV), hg_w_o[0].astype(BF16), x.reshape(T, D),
                     ln_mix_g[0], ln_mix_b[0])
    h = _peer_layer(h, peer_w_q[0], peer_sub_keys[0], peer_u[0], peer_v[0],
                    ln_ffn_g[0], ln_ffn_b[0])

    k_dup, v_dup = _shared_kv(h, w_kv, pos, tab)
    h = _swa_layer(h, k_dup, v_dup, pos, tab, sw_w_q[0].astype(BF16), sw_sinks[0].astype(F32),
                   sw_w_o[0].astype(BF16), ln_mix_g[1], ln_mix_b[1], batch=B)
    h = _peer_layer(h, peer_w_q[1], peer_sub_keys[1], peer_u[1], peer_v[1],
                    ln_ffn_g[1], ln_ffn_b[1])
    return h.reshape(B, S, D)
```

```python
import functools
import math

import jax
import jax.numpy as jnp
from jax import lax
from jax.experimental import pallas as pl
from jax.experimental.pallas import tpu as pltpu

F32 = jnp.float32
BF16 = jnp.bfloat16

D_MODEL = 1024
DEPTH = 2
HG_HEADS = 8
HG_DK = 128
HG_DV = 128
HG_CHUNK = 32
SW_Q_HEADS = 16
SW_KV_HEADS = 4
SW_HEAD_DIM = 64
SW_WINDOW = 128
ROPE_THETA = 500000.0
ROPE_DIMS = 16
PEER_HEADS = 8
PEER_NKEYS = 128
PEER_HALF = 128
PEER_TOPK = 16
DN_ALPHA = (2.0 * DEPTH) ** 0.25
LN_EPS = 1e-5
RMS_EPS = 1e-6

LANES = 128
VMEM_LIMIT = 56 * 1024 * 1024
NEG_INF = float("-inf")
INV_SQRT2 = 1.0 / math.sqrt(2.0)


def _sigmoid(x):
    return 1.0 / (1.0 + jnp.exp(-x))


def _nt_dot(a, b):
    return lax.dot_general(a, b, (((1,), (1,)), ((), ())), preferred_element_type=F32)


def _tn_dot(a, b):
    return lax.dot_general(a, b, (((0,), (0,)), ((), ())), preferred_element_type=F32)


def _hgrn2_kernel(x_ref, w_ref, lb_ref, nw_ref, o_ref, proj_ref, *, seq, chunk, row_tile):
    def project(i, carry):
        r = pl.ds(pl.multiple_of(i * row_tile, row_tile), row_tile)
        proj_ref[r, :] = jnp.dot(x_ref[0, r, :], w_ref[0], preferred_element_type=F32)
        return carry

    lax.fori_loop(0, seq // row_tile, project, 0)

    lb = lb_ref[0]
    nw = nw_ref[0]
    row = lax.broadcasted_iota(jnp.int32, (chunk, HG_DK), 0)
    causal = (lax.broadcasted_iota(jnp.int32, (chunk, chunk), 1)
              <= lax.broadcasted_iota(jnp.int32, (chunk, chunk), 0))

    def step(c, state_t):
        r = pl.ds(pl.multiple_of(c * chunk, chunk), chunk)
        q = proj_ref[r, 0:128]
        f = proj_ref[r, 128:256]
        v = proj_ref[r, 256:384]
        g = proj_ref[r, 384:512]
        q = q * _sigmoid(q)
        fg = lb + (1.0 - lb) * _sigmoid(f)
        k = 1.0 - fg
        b = jnp.log(fg)
        s = 1
        while s < chunk:
            b = b + jnp.where(row >= s, pltpu.roll(b, s, axis=0), 0.0)
            s *= 2
        b_last = b[chunk - 1:chunk, :]
        qd = (q * jnp.exp(b)).astype(BF16)
        kd = (k * jnp.exp(-b)).astype(BF16)
        kl = (k * jnp.exp(b_last - b)).astype(BF16)
        vb = v.astype(BF16)
        att = jnp.where(causal, _nt_dot(qd, kd), 0.0)
        o = (jnp.dot(att.astype(BF16), vb, preferred_element_type=F32)
             + _nt_dot(qd, state_t.astype(BF16)))
        state_t = state_t * jnp.exp(b_last) + _tn_dot(vb, kl)
        o = o * lax.rsqrt(jnp.mean(o * o, axis=-1, keepdims=True) + RMS_EPS)
        o_ref[0, r, :] = o * nw * (g * _sigmoid(g))
        return state_t

    lax.fori_loop(0, seq // chunk, step, jnp.zeros((HG_DV, HG_DK), F32), unroll=2)


def _hgrn2_mix(xb, w_heads, lb, norm_w):
    B, S, D = xb.shape
    H = w_heads.shape[0]
    kern = functools.partial(_hgrn2_kernel, seq=S, chunk=HG_CHUNK, row_tile=min(512, S))
    return pl.pallas_call(
        kern,
        grid=(B, H),
        in_specs=[
            pl.BlockSpec((1, S, D), lambda b, h: (b, 0, 0)),
            pl.BlockSpec((1, D, 4 * HG_DK), lambda b, h: (h, 0, 0)),
            pl.BlockSpec((1, 1, HG_DK), lambda b, h: (h, 0, 0)),
            pl.BlockSpec((1, 1, HG_DV), lambda b, h: (h, 0, 0)),
        ],
        out_specs=pl.BlockSpec((1, S, HG_DV), lambda b, h: (b, 0, h)),
        out_shape=jax.ShapeDtypeStruct((B, S, H * HG_DV), F32),
        scratch_shapes=[pltpu.VMEM((S, 4 * HG_DK), F32)],
        compiler_params=pltpu.CompilerParams(
            dimension_semantics=("parallel", "arbitrary"), vmem_limit_bytes=VMEM_LIMIT),
        name="hgrn2_mix",
    )(xb, w_heads, lb, norm_w)


def _layernorm_rows(z, g, b):
    mu = jnp.mean(z, axis=-1, keepdims=True)
    zc = z - mu
    var = jnp.mean(zc * zc, axis=-1, keepdims=True)
    return zc * lax.rsqrt(var + LN_EPS) * g + b


def _proj_res_ln_kernel(a_ref, w_ref, res_ref, g_ref, b_ref, o_ref):
    y = jnp.dot(a_ref[...].astype(BF16), w_ref[...], preferred_element_type=F32)
    o_ref[...] = _layernorm_rows(DN_ALPHA * res_ref[...] + y, g_ref[...], b_ref[...])


def _proj_res_ln(a, w_bf16, res, g, b, *, tm=512):
    T, K = a.shape
    N = w_bf16.shape[1]
    tm = min(tm, T)
    return pl.pallas_call(
        _proj_res_ln_kernel,
        grid=(T // tm,),
        in_specs=[
            pl.BlockSpec((tm, K), lambda i: (i, 0)),
            pl.BlockSpec((K, N), lambda i: (0, 0)),
            pl.BlockSpec((tm, N), lambda i: (i, 0)),
            pl.BlockSpec((1, N), lambda i: (0, 0)),
            pl.BlockSpec((1, N), lambda i: (0, 0)),
        ],
        out_specs=pl.BlockSpec((tm, N), lambda i: (i, 0)),
        out_shape=jax.ShapeDtypeStruct((T, N), F32),
        compiler_params=pltpu.CompilerParams(
            dimension_semantics=("parallel",), vmem_limit_bytes=VMEM_LIMIT),
        name="proj_res_ln",
    )(a, w_bf16, res, g.reshape(1, N), b.reshape(1, N))


def _top16_desc(s):
    vals = []
    rank = jnp.full(s.shape, 255.0, F32)
    for q in range(PEER_TOPK):
        m = jnp.max(s, axis=0, keepdims=True)
        hit = s == m
        rank = jnp.where(hit, float(q), rank)
        s = jnp.where(hit, NEG_INF, s)
        vals.append(m)
    return vals, rank


def _rows_to_slab(vals, lo):
    n = vals[0].shape[1]
    rid = lax.broadcasted_iota(jnp.int32, (8, n), 0)
    slab = jnp.zeros((8, n), F32)
    for r in range(8):
        slab = jnp.where(rid == r, vals[lo + r], slab)
    return slab


def _peer_route_kernel(ht_ref, wq_ref, sk_ref, ga_ref, cnt_ref, gb_ref, rk_ref, q_scr, s_scr, *, tm):
    hb = ht_ref[...].astype(BF16)
    q_scr[...] = jnp.dot(wq_ref[...], hb, preferred_element_type=F32).astype(BF16)
    for hc in range(2 * PEER_HEADS):
        s_scr[hc] = jnp.dot(sk_ref[hc], q_scr[hc * PEER_HALF:(hc + 1) * PEER_HALF, :],
                            preferred_element_type=F32)

    ncol = tm // LANES
    rid8 = lax.broadcasted_iota(jnp.int32, (8, LANES), 0)

    def body(it, carry):
        hh = it // ncol
        col = pl.ds(pl.multiple_of((it % ncol) * LANES, LANES), LANES)
        s1 = s_scr[2 * hh, :, col]
        s2 = s_scr[2 * hh + 1, :, col]
        a_vals, _ = _top16_desc(s1)
        b_vals, rank2 = _top16_desc(s2)
        a_lo = _rows_to_slab(a_vals, 0)
        a_hi = _rows_to_slab(a_vals, 8)
        slabs = [a_lo + b_vals[0], a_hi + b_vals[0]]
        for q in range(1, PEER_TOPK):
            keep = PEER_TOPK // (q + 1)
            slabs.append(jnp.where(rid8 < keep, a_lo + b_vals[q], NEG_INF))
        cand = jnp.concatenate(slabs, axis=0)
        for _ in range(PEER_TOPK - 1):
            m = jnp.max(cand, axis=0, keepdims=True)
            cand = jnp.where(cand == m, NEG_INF, cand)
        thr = jnp.max(cand, axis=0, keepdims=True)

        ea_lo = jnp.exp(a_lo - a_vals[0])
        ea_hi = jnp.exp(a_hi - a_vals[0])
        cnt = jnp.zeros(s1.shape, F32)
        zacc = jnp.zeros((8, LANES), F32)
        for q in range(PEER_TOPK):
            bq = b_vals[q]
            cnt = cnt + jnp.where(s1 + bq >= thr, 1.0, 0.0)
            part = (jnp.where(a_lo + bq >= thr, ea_lo, 0.0)
                    + jnp.where(a_hi + bq >= thr, ea_hi, 0.0))
            zacc = zacc + jnp.exp(bq - b_vals[0]) * part
        inv_z = 1.0 / jnp.sum(zacc, axis=0, keepdims=True)

        ga_ref[hh, :, col] = jnp.exp(s1 - a_vals[0])
        cnt_ref[hh, :, col] = cnt
        gb_ref[hh, :, col] = jnp.exp(s2 - b_vals[0]) * inv_z
        rk_ref[hh, :, col] = rank2
        return carry

    lax.fori_loop(0, PEER_HEADS * ncol, body, 0)


def _peer_route(ht, wq_t, sub_keys, *, tm=512):
    D, T = ht.shape
    tm = min(tm, T)
    nq = wq_t.shape[0]
    kern = functools.partial(_peer_route_kernel, tm=tm)
    head_rows = pl.BlockSpec((PEER_HEADS, PEER_NKEYS, tm), lambda i: (0, 0, i))
    shp = (PEER_HEADS, PEER_NKEYS, T)
    return pl.pallas_call(
        kern,
        grid=(T // tm,),
        in_specs=[
            pl.BlockSpec((D, tm), lambda i: (0, i)),
            pl.BlockSpec((nq, D), lambda i: (0, 0)),
            pl.BlockSpec((2 * PEER_HEADS, PEER_NKEYS, PEER_HALF), lambda i: (0, 0, 0)),
        ],
        out_specs=[head_rows, head_rows, head_rows, head_rows],
        out_shape=[jax.ShapeDtypeStruct(shp, F32), jax.ShapeDtypeStruct(shp, F32),
                   jax.ShapeDtypeStruct(shp, F32), jax.ShapeDtypeStruct(shp, F32)],
        scratch_shapes=[pltpu.VMEM((nq, tm), BF16),
                        pltpu.VMEM((2 * PEER_HEADS, PEER_NKEYS, tm), F32)],
        compiler_params=pltpu.CompilerParams(
            dimension_semantics=("parallel",), vmem_limit_bytes=VMEM_LIMIT),
        name="peer_route",
    )(ht, wq_t, sub_keys)


def _peer_expert_kernel(xb_ref, u_ref, vt_ref, ga_ref, cnt_ref, gb_ref, rk_ref, res_ref, g_ref,
                        b_ref, o_ref, a_scr, w_scr, rk16_scr, gb16_scr, *, tm, ni):
    e = pl.program_id(1)

    @pl.when(e == 0)
    def _():
        o_ref[...] = jnp.zeros(o_ref.shape, F32)

        def pack_head(h, carry):
            rk16_scr[h] = rk_ref[h].astype(BF16)
            gb16_scr[h] = gb_ref[h].astype(BF16)
            return carry

        lax.fori_loop(0, PEER_HEADS, pack_head, 0)

    group = 4
    slab = 16
    nslab = PEER_NKEYS // slab
    wide = 2 * LANES

    def routing_weight(a_rd, w_wr, col):
        for g0 in range(0, ni, group):
            accs = [[jnp.zeros((slab, LANES), BF16) for _ in range(nslab)] for _ in range(group)]
            for h in range(PEER_HEADS):
                rk = rk16_scr[h, :, col]
                gb = gb16_scr[h, :, col]
                rks = [rk[i * slab:(i + 1) * slab] for i in range(nslab)]
                gbs = [gb[i * slab:(i + 1) * slab] for i in range(nslab)]
                for k in range(group):
                    il = g0 + k
                    n16 = jnp.broadcast_to(cnt_ref[h, il:il + 1, col], (slab, LANES)).astype(BF16)
                    a16 = jnp.broadcast_to(ga_ref[h, il:il + 1, col], (slab, LANES)).astype(BF16)
                    for i in range(nslab):
                        accs[k][i] = accs[k][i] + jnp.where(rks[i] < n16, gbs[i] * a16,
                                                            jnp.zeros_like(a16))
            for k in range(group):
                r0 = (g0 + k) * PEER_NKEYS
                a = a_rd[r0:r0 + PEER_NKEYS, col].astype(BF16)
                gelu = (0.5 * a) * (1.0 + lax.erf(a * INV_SQRT2))
                w_wr[r0:r0 + PEER_NKEYS, col] = jnp.concatenate(accs[k], axis=0) * gelu

    a_scr[...] = jnp.dot(u_ref[...], xb_ref[...], preferred_element_type=F32)

    def vpu_pass(cc, carry):
        routing_weight(a_scr, w_scr, pl.ds(pl.multiple_of(cc * LANES, LANES), LANES))
        return carry

    lax.fori_loop(0, tm // LANES, vpu_pass, 0)
    o_ref[...] += jnp.dot(vt_ref[...], w_scr[...], preferred_element_type=F32)

    @pl.when(e == pl.num_programs(1) - 1)
    def _():
        def ln_cols(cc, carry):
            col = pl.ds(pl.multiple_of(cc * LANES, LANES), LANES)
            z = DN_ALPHA * res_ref[:, col] + o_ref[:, col]
            mu = jnp.mean(z, axis=0, keepdims=True)
            zc = z - mu
            var = jnp.mean(zc * zc, axis=0, keepdims=True)
            o_ref[:, col] = zc * lax.rsqrt(var + LN_EPS) * g_ref[...] + b_ref[...]
            return carry

        lax.fori_loop(0, tm // LANES, ln_cols, 0)


def _peer_experts(ht, u_bf16, vt_bf16, ga, cnt, gb, rk, g, b, *, tm=1024, eb=1024):
    D, T = ht.shape
    E = u_bf16.shape[0]
    tm = min(tm, T)
    ni = eb // PEER_NKEYS
    kern = functools.partial(_peer_expert_kernel, tm=tm, ni=ni)
    once = pl.Buffered(1)
    return pl.pallas_call(
        kern,
        grid=(T // tm, E // eb),
        in_specs=[
            pl.BlockSpec((D, tm), lambda t, e: (0, t), pipeline_mode=once),
            pl.BlockSpec((eb, D), lambda t, e: (e, 0)),
            pl.BlockSpec((D, eb), lambda t, e: (0, e)),
            pl.BlockSpec((PEER_HEADS, ni, tm), lambda t, e: (0, e, t)),
            pl.BlockSpec((PEER_HEADS, ni, tm), lambda t, e: (0, e, t)),
            pl.BlockSpec((PEER_HEADS, PEER_NKEYS, tm), lambda t, e: (0, 0, t), pipeline_mode=once),
            pl.BlockSpec((PEER_HEADS, PEER_NKEYS, tm), lambda t, e: (0, 0, t), pipeline_mode=once),
            pl.BlockSpec((D, tm), lambda t, e: (0, t), pipeline_mode=once),
            pl.BlockSpec((D, 1), lambda t, e: (0, 0)),
            pl.BlockSpec((D, 1), lambda t, e: (0, 0)),
        ],
        out_specs=pl.BlockSpec((D, tm), lambda t, e: (0, t)),
        out_shape=jax.ShapeDtypeStruct((D, T), F32),
        scratch_shapes=[pltpu.VMEM((eb, tm), F32), pltpu.VMEM((eb, tm), BF16),
                        pltpu.VMEM((PEER_HEADS, PEER_NKEYS, tm), BF16),
                        pltpu.VMEM((PEER_HEADS, PEER_NKEYS, tm), BF16)],
        compiler_params=pltpu.CompilerParams(
            dimension_semantics=("parallel", "arbitrary"), vmem_limit_bytes=VMEM_LIMIT),
        name="peer_experts",
    )(ht.astype(BF16), u_bf16, vt_bf16, ga, cnt, gb, rk, ht, g.reshape(D, 1), b.reshape(D, 1))


def _peer_layer(h, w_pq, sub_keys, u_tab, v_tab, g, b):
    ht = h.T
    wq_t = w_pq.T.astype(BF16)
    sk = sub_keys.reshape(2 * PEER_HEADS, PEER_NKEYS, PEER_HALF).astype(BF16)
    ga, cnt, gb, rk = _peer_route(ht, wq_t, sk)
    out_t = _peer_experts(ht, u_tab.astype(BF16), v_tab.T.astype(BF16), ga, cnt, gb, rk, g, b)
    return out_t.T


def _rope_lane_tables():
    d = jnp.arange(LANES) % SW_HEAD_DIM
    half = ROPE_DIMS // 2
    inv = ROPE_THETA ** (-(2.0 * (d % half).astype(F32)) / ROPE_DIMS)
    inv = jnp.where(d < ROPE_DIMS, inv, 0.0)
    sgn_up = jnp.where(d < half, -1.0, 0.0)
    sgn_dn = jnp.where((d >= half) & (d < ROPE_DIMS), 1.0, 0.0)
    return jnp.stack([inv, sgn_up, sgn_dn]).astype(F32)


def _rope_apply(x, pos_f, tab):
    n, width = x.shape
    reps = width // LANES
    ang = pos_f * tab[0:1, :]
    cos = jnp.tile(jnp.cos(ang), (1, reps))
    sin = jnp.tile(jnp.sin(ang), (1, reps))
    s_up = jnp.tile(tab[1:2, :], (1, reps))
    s_dn = jnp.tile(tab[2:3, :], (1, reps))
    half = ROPE_DIMS // 2
    partner = pltpu.roll(x, width - half, axis=1) * s_up + pltpu.roll(x, half, axis=1) * s_dn
    return x * cos + partner * sin


def _kv_kernel(h_ref, w_ref, pos_ref, tab_ref, k_ref, v_ref):
    kv = jnp.dot(h_ref[...].astype(BF16), w_ref[...], preferred_element_type=F32)
    nk2 = SW_KV_HEADS * LANES
    k_ref[...] = _rope_apply(kv[:, :nk2], pos_ref[...].astype(F32), tab_ref[...]).astype(BF16)
    v_ref[...] = kv[:, nk2:].astype(BF16)


def _dup_heads(w):
    D, n = w.shape
    w3 = w.reshape(D, n // SW_HEAD_DIM, SW_HEAD_DIM)
    return jnp.concatenate([w3, w3], axis=2).reshape(D, 2 * n)


def _shared_kv(h, w_kv, pos, tab, *, tm=512):
    T, D = h.shape
    tm = min(tm, T)
    nk = SW_KV_HEADS * SW_HEAD_DIM
    nk2 = SW_KV_HEADS * LANES
    w_kv_bf16 = jnp.concatenate([_dup_heads(w_kv[:, :nk]), _dup_heads(w_kv[:, nk:])],
                                axis=1).astype(BF16)
    return pl.pallas_call(
        _kv_kernel,
        grid=(T // tm,),
        in_specs=[
            pl.BlockSpec((tm, D), lambda i: (i, 0)),
            pl.BlockSpec(w_kv_bf16.shape, lambda i: (0, 0)),
            pl.BlockSpec((tm, 1), lambda i: (i, 0)),
            pl.BlockSpec((3, LANES), lambda i: (0, 0)),
        ],
        out_specs=[pl.BlockSpec((tm, nk2), lambda i: (i, 0)),
                   pl.BlockSpec((tm, nk2), lambda i: (i, 0))],
        out_shape=[jax.ShapeDtypeStruct((T, nk2), BF16), jax.ShapeDtypeStruct((T, nk2), BF16)],
        compiler_params=pltpu.CompilerParams(
            dimension_semantics=("parallel",), vmem_limit_bytes=VMEM_LIMIT),
        name="shared_kv",
    )(h, w_kv_bf16, pos, tab)


def _swa_kernel(sink_ref, h_ref, wq_ref, k_ref, v_ref, pos_ref, tab_ref, wo_ref, g_ref, b_ref,
                o_ref, q_scr, att_scr, *, tq, seq):
    j = pl.program_id(1)
    W = SW_WINDOW
    hq = h_ref[...]
    q = jnp.dot(hq.astype(BF16), wq_ref[...], preferred_element_type=F32)
    q = _rope_apply(q, pos_ref[...].astype(F32), tab_ref[...]) * (SW_HEAD_DIM ** -0.5)
    q_scr[...] = q.astype(BF16)

    qi = lax.broadcasted_iota(jnp.int32, (W, 2 * W), 0)
    kj = lax.broadcasted_iota(jnp.int32, (W, 2 * W), 1)
    lane = lax.broadcasted_iota(jnp.int32, (W, LANES), 1)
    first_half = lane < SW_HEAD_DIM
    group = SW_Q_HEADS // SW_KV_HEADS

    def block(n, carry):
        q0 = j * tq + n * W
        k0 = jnp.maximum(q0 - W, 0)
        rows = pl.ds(pl.multiple_of(n * W, W), W)
        band = pl.ds(pl.multiple_of(k0, W), 2 * W)
        diff = (q0 + qi) - (k0 + kj)
        valid = (diff >= 0) & (diff < W)
        for pair in range(SW_Q_HEADS // 2):
            kvh = (2 * pair) // group
            lanes = slice(pair * LANES, (pair + 1) * LANES)
            qp = q_scr[rows, lanes]
            kb = k_ref[0, band, kvh * LANES:(kvh + 1) * LANES]
            vb = v_ref[0, band, kvh * LANES:(kvh + 1) * LANES]
            outs = []
            for sub in range(2):
                head = 2 * pair + sub
                keep = first_half if sub == 0 else jnp.logical_not(first_half)
                qh = jnp.where(keep, qp, jnp.zeros_like(qp))
                logits = jnp.where(valid, _nt_dot(qh, kb), NEG_INF)
                sink = sink_ref[head]
                m = jnp.maximum(jnp.max(logits, axis=-1, keepdims=True), sink)
                p = jnp.exp(logits - m)
                denom = jnp.sum(p, axis=-1, keepdims=True) + jnp.exp(sink - m)
                p = p * (1.0 / denom)
                outs.append(jnp.dot(p.astype(BF16), vb, preferred_element_type=F32))
            att_scr[rows, lanes] = jnp.where(first_half, outs[0], outs[1]).astype(BF16)
        return carry

    lax.fori_loop(0, tq // W, block, 0)
    y = jnp.dot(att_scr[...], wo_ref[...], preferred_element_type=F32)
    o_ref[...] = _layernorm_rows(DN_ALPHA * hq + y, g_ref[...], b_ref[...])


def _swa_layer(h, k_dup, v_dup, pos, tab, w_q_bf16, sinks, w_o_bf16, g, b, *, batch, tq=512):
    T, D = h.shape
    S = T // batch
    tq = min(tq, S)
    nq = S // tq
    nk2 = k_dup.shape[1]
    k3 = k_dup.reshape(batch, S, nk2)
    v3 = v_dup.reshape(batch, S, nk2)
    kern = functools.partial(_swa_kernel, tq=tq, seq=S)
    return pl.pallas_call(
        kern,
        grid=(batch, nq),
        in_specs=[
            pl.BlockSpec(memory_space=pltpu.SMEM),
            pl.BlockSpec((tq, D), lambda bb, j: (bb * nq + j, 0)),
            pl.BlockSpec((D, D), lambda bb, j: (0, 0)),
            pl.BlockSpec((1, S, nk2), lambda bb, j: (bb, 0, 0)),
            pl.BlockSpec((1, S, nk2), lambda bb, j: (bb, 0, 0)),
            pl.BlockSpec((tq, 1), lambda bb, j: (bb * nq + j, 0)),
            pl.BlockSpec((3, LANES), lambda bb, j: (0, 0)),
            pl.BlockSpec((D, D), lambda bb, j: (0, 0)),
            pl.BlockSpec((1, D), lambda bb, j: (0, 0)),
            pl.BlockSpec((1, D), lambda bb, j: (0, 0)),
        ],
        out_specs=pl.BlockSpec((tq, D), lambda bb, j: (bb * nq + j, 0)),
        out_shape=jax.ShapeDtypeStruct((T, D), F32),
        scratch_shapes=[pltpu.VMEM((tq, D), BF16), pltpu.VMEM((tq, D), BF16)],
        compiler_params=pltpu.CompilerParams(
            dimension_semantics=("parallel", "arbitrary"), vmem_limit_bytes=VMEM_LIMIT),
        name="swa_layer",
    )(sinks, h, w_q_bf16, k3, v3, pos, tab, w_o_bf16, g.reshape(1, D), b.reshape(1, D))


def kernel(x, positions, hg_w_in, hg_lb, hg_norm_w, hg_w_o, sw_w_q, sw_sinks, sw_w_o, w_kv,
           peer_w_q, peer_sub_keys, peer_u, peer_v, ln_mix_g, ln_mix_b, ln_ffn_g, ln_ffn_b):
    B, S, D = x.shape
    T = B * S
    H = HG_HEADS
    pos = positions.reshape(T, 1)
    tab = _rope_lane_tables()

    lb_all = jnp.cumsum(jax.nn.softmax(hg_lb.astype(F32), axis=0), axis=0)
    w_heads = (hg_w_in[0].reshape(D, 4, H, HG_DK).transpose(2, 0, 1, 3)
               .reshape(H, D, 4 * HG_DK).astype(BF16))
    mix = _hgrn2_mix(x.astype(BF16), w_heads, lb_all[0].reshape(H, 1, HG_DK),
                     hg_norm_w[0].reshape(H, 1, HG_DV))
    h = _proj_res_ln(mix.reshape(T, H * HG_DV), hg_w_o[0].astype(BF16), x.reshape(T, D),
                     ln_mix_g[0], ln_mix_b[0])
    h = _peer_layer(h, peer_w_q[0], peer_sub_keys[0], peer_u[0], peer_v[0],
                    ln_ffn_g[0], ln_ffn_b[0])

    k_dup, v_dup = _shared_kv(h, w_kv, pos, tab)
    h = _swa_layer(h, k_dup, v_dup, pos, tab, sw_w_q[0].astype(BF16), sw_sinks[0].astype(F32),
                   sw_w_o[0].astype(BF16), ln_mix_g[1], ln_mix_b[1], batch=B)
    h = _peer_layer(h, peer_w_q[1], peer_sub_keys[1], peer_u[1], peer_v[1],
                    ln_ffn_g[1], ln_ffn_b[1])
    return h.reshape(B, S, D)
```

```python
import functools
import math

import jax
import jax.numpy as jnp
from jax import lax
from jax.experimental import pallas as pl
from jax.experimental.pallas import tpu as pltpu

F32 = jnp.float32
BF16 = jnp.bfloat16

D_MODEL = 1024
DEPTH = 2
HG_HEADS = 8
HG_DK = 128
HG_DV = 128
HG_CHUNK = 32
SW_Q_HEADS = 16
SW_KV_HEADS = 4
SW_HEAD_DIM = 64
SW_WINDOW = 128
ROPE_THETA = 500000.0
ROPE_DIMS = 16
PEER_HEADS = 8
PEER_NKEYS = 128
PEER_HALF = 128
PEER_TOPK = 16
DN_ALPHA = (2.0 * DEPTH) ** 0.25
LN_EPS = 1e-5
RMS_EPS = 1e-6

LANES = 128
VMEM_LIMIT = 56 * 1024 * 1024
NEG_INF = float("-inf")
INV_SQRT2 = 1.0 / math.sqrt(2.0)


def _sigmoid(x):
    return 1.0 / (1.0 + jnp.exp(-x))


def _nt_dot(a, b):
    return lax.dot_general(a, b, (((1,), (1,)), ((), ())), preferred_element_type=F32)


def _tn_dot(a, b):
    return lax.dot_general(a, b, (((0,), (0,)), ((), ())), preferred_element_type=F32)


def _hgrn2_kernel(x_ref, w_ref, lb_ref, nw_ref, o_ref, proj_ref, *, seq, chunk, row_tile):
    def project(i, carry):
        r = pl.ds(pl.multiple_of(i * row_tile, row_tile), row_tile)
        proj_ref[r, :] = jnp.dot(x_ref[0, r, :], w_ref[0], preferred_element_type=F32)
        return carry

    lax.fori_loop(0, seq // row_tile, project, 0)

    lb = lb_ref[0]
    nw = nw_ref[0]
    row = lax.broadcasted_iota(jnp.int32, (chunk, HG_DK), 0)
    causal = (lax.broadcasted_iota(jnp.int32, (chunk, chunk), 1)
              <= lax.broadcasted_iota(jnp.int32, (chunk, chunk), 0))

    def step(c, state_t):
        r = pl.ds(pl.multiple_of(c * chunk, chunk), chunk)
        q = proj_ref[r, 0:128]
        f = proj_ref[r, 128:256]
        v = proj_ref[r, 256:384]
        g = proj_ref[r, 384:512]
        q = q * _sigmoid(q)
        fg = lb + (1.0 - lb) * _sigmoid(f)
        k = 1.0 - fg
        b = jnp.log(fg)
        s = 1
        while s < chunk:
            b = b + jnp.where(row >= s, pltpu.roll(b, s, axis=0), 0.0)
            s *= 2
        b_last = b[chunk - 1:chunk, :]
        qd = (q * jnp.exp(b)).astype(BF16)
        kd = (k * jnp.exp(-b)).astype(BF16)
        kl = (k * jnp.exp(b_last - b)).astype(BF16)
        vb = v.astype(BF16)
        att = jnp.where(causal, _nt_dot(qd, kd), 0.0)
        o = (jnp.dot(att.astype(BF16), vb, preferred_element_type=F32)
             + _nt_dot(qd, state_t.astype(BF16)))
        state_t = state_t * jnp.exp(b_last) + _tn_dot(vb, kl)
        o = o * lax.rsqrt(jnp.mean(o * o, axis=-1, keepdims=True) + RMS_EPS)
        o_ref[0, r, :] = o * nw * (g * _sigmoid(g))
        return state_t

    lax.fori_loop(0, seq // chunk, step, jnp.zeros((HG_DV, HG_DK), F32), unroll=2)


def _hgrn2_mix(xb, w_heads, lb, norm_w):
    B, S, D = xb.shape
    H = w_heads.shape[0]
    kern = functools.partial(_hgrn2_kernel, seq=S, chunk=HG_CHUNK, row_tile=min(512, S))
    return pl.pallas_call(
        kern,
        grid=(B, H),
        in_specs=[
            pl.BlockSpec((1, S, D), lambda b, h: (b, 0, 0)),
            pl.BlockSpec((1, D, 4 * HG_DK), lambda b, h: (h, 0, 0)),
            pl.BlockSpec((1, 1, HG_DK), lambda b, h: (h, 0, 0)),
            pl.BlockSpec((1, 1, HG_DV), lambda b, h: (h, 0, 0)),
        ],
        out_specs=pl.BlockSpec((1, S, HG_DV), lambda b, h: (b, 0, h)),
        out_shape=jax.ShapeDtypeStruct((B, S, H * HG_DV), F32),
        scratch_shapes=[pltpu.VMEM((S, 4 * HG_DK), F32)],
        compiler_params=pltpu.CompilerParams(
            dimension_semantics=("parallel", "arbitrary"), vmem_limit_bytes=VMEM_LIMIT),
        name="hgrn2_mix",
    )(xb, w_heads, lb, norm_w)


def _layernorm_rows(z, g, b):
    mu = jnp.mean(z, axis=-1, keepdims=True)
    zc = z - mu
    var = jnp.mean(zc * zc, axis=-1, keepdims=True)
    return zc * lax.rsqrt(var + LN_EPS) * g + b


def _proj_res_ln_kernel(a_ref, w_ref, res_ref, g_ref, b_ref, o_ref):
    y = jnp.dot(a_ref[...].astype(BF16), w_ref[...], preferred_element_type=F32)
    o_ref[...] = _layernorm_rows(DN_ALPHA * res_ref[...] + y, g_ref[...], b_ref[...])


def _proj_res_ln(a, w_bf16, res, g, b, *, tm=512):
    T, K = a.shape
    N = w_bf16.shape[1]
    tm = min(tm, T)
    return pl.pallas_call(
        _proj_res_ln_kernel,
        grid=(T // tm,),
        in_specs=[
            pl.BlockSpec((tm, K), lambda i: (i, 0)),
            pl.BlockSpec((K, N), lambda i: (0, 0)),
            pl.BlockSpec((tm, N), lambda i: (i, 0)),
            pl.BlockSpec((1, N), lambda i: (0, 0)),
            pl.BlockSpec((1, N), lambda i: (0, 0)),
        ],
        out_specs=pl.BlockSpec((tm, N), lambda i: (i, 0)),
        out_shape=jax.ShapeDtypeStruct((T, N), F32),
        compiler_params=pltpu.CompilerParams(
            dimension_semantics=("parallel",), vmem_limit_bytes=VMEM_LIMIT),
        name="proj_res_ln",
    )(a, w_bf16, res, g.reshape(1, N), b.reshape(1, N))


def _top16_desc(s):
    vals = []
    rank = jnp.full(s.shape, 255.0, F32)
    for q in range(PEER_TOPK):
        m = jnp.max(s, axis=0, keepdims=True)
        hit = s == m
        rank = jnp.where(hit, float(q), rank)
        s = jnp.where(hit, NEG_INF, s)
        vals.append(m)
    return vals, rank


def _rows_to_slab(vals, lo):
    n = vals[0].shape[1]
    rid = lax.broadcasted_iota(jnp.int32, (8, n), 0)
    slab = jnp.zeros((8, n), F32)
    for r in range(8):
        slab = jnp.where(rid == r, vals[lo + r], slab)
    return slab


def _peer_route_kernel(ht_ref, wq_ref, sk_ref, ga_ref, cnt_ref, gb_ref, rk_ref, q_scr, s_scr, *, tm):
    hb = ht_ref[...].astype(BF16)
    q_scr[...] = jnp.dot(wq_ref[...], hb, preferred_element_type=F32).astype(BF16)
    for hc in range(2 * PEER_HEADS):
        s_scr[hc] = jnp.dot(sk_ref[hc], q_scr[hc * PEER_HALF:(hc + 1) * PEER_HALF, :],
                            preferred_element_type=F32)

    ncol = tm // LANES
    rid8 = lax.broadcasted_iota(jnp.int32, (8, LANES), 0)

    def body(it, carry):
        hh = it // ncol
        col = pl.ds(pl.multiple_of((it % ncol) * LANES, LANES), LANES)
        s1 = s_scr[2 * hh, :, col]
        s2 = s_scr[2 * hh + 1, :, col]
        a_vals, _ = _top16_desc(s1)
        b_vals, rank2 = _top16_desc(s2)
        a_lo = _rows_to_slab(a_vals, 0)
        a_hi = _rows_to_slab(a_vals, 8)
        slabs = [a_lo + b_vals[0], a_hi + b_vals[0]]
        for q in range(1, PEER_TOPK):
            keep = PEER_TOPK // (q + 1)
            slabs.append(jnp.where(rid8 < keep, a_lo + b_vals[q], NEG_INF))
        cand = jnp.concatenate(slabs, axis=0)
        for _ in range(PEER_TOPK - 1):
            m = jnp.max(cand, axis=0, keepdims=True)
            cand = jnp.where(cand == m, NEG_INF, cand)
        thr = jnp.max(cand, axis=0, keepdims=True)

        ea_lo = jnp.exp(a_lo - a_vals[0])
        ea_hi = jnp.exp(a_hi - a_vals[0])
        cnt = jnp.zeros(s1.shape, F32)
        zacc = jnp.zeros((8, LANES), F32)
        for q in range(PEER_TOPK):
            bq = b_vals[q]
            cnt = cnt + jnp.where(s1 + bq >= thr, 1.0, 0.0)
            part = (jnp.where(a_lo + bq >= thr, ea_lo, 0.0)
                    + jnp.where(a_hi + bq >= thr, ea_hi, 0.0))
            zacc = zacc + jnp.exp(bq - b_vals[0]) * part
        inv_z = 1.0 / jnp.sum(zacc, axis=0, keepdims=True)

        ga_ref[hh, :, col] = jnp.exp(s1 - a_vals[0])
        cnt_ref[hh, :, col] = cnt
        gb_ref[hh, :, col] = (jnp.exp(s2 - b_vals[0]) * inv_z).astype(BF16)
        rk_ref[hh, :, col] = rank2.astype(BF16)
        return carry

    lax.fori_loop(0, PEER_HEADS * ncol, body, 0)


def _peer_route(ht, wq_t, sub_keys, *, tm=512):
    D, T = ht.shape
    tm = min(tm, T)
    nq = wq_t.shape[0]
    kern = functools.partial(_peer_route_kernel, tm=tm)
    head_rows = pl.BlockSpec((PEER_HEADS, PEER_NKEYS, tm), lambda i: (0, 0, i))
    shp = (PEER_HEADS, PEER_NKEYS, T)
    return pl.pallas_call(
        kern,
        grid=(T // tm,),
        in_specs=[
            pl.BlockSpec((D, tm), lambda i: (0, i)),
            pl.BlockSpec((nq, D), lambda i: (0, 0)),
            pl.BlockSpec((2 * PEER_HEADS, PEER_NKEYS, PEER_HALF), lambda i: (0, 0, 0)),
        ],
        out_specs=[head_rows, head_rows, head_rows, head_rows],
        out_shape=[jax.ShapeDtypeStruct(shp, F32), jax.ShapeDtypeStruct(shp, F32),
                   jax.ShapeDtypeStruct(shp, BF16), jax.ShapeDtypeStruct(shp, BF16)],
        scratch_shapes=[pltpu.VMEM((nq, tm), BF16),
                        pltpu.VMEM((2 * PEER_HEADS, PEER_NKEYS, tm), F32)],
        compiler_params=pltpu.CompilerParams(
            dimension_semantics=("parallel",), vmem_limit_bytes=VMEM_LIMIT),
        name="peer_route",
    )(ht, wq_t, sub_keys)


def _peer_expert_kernel(res_ref, u_ref, vt_ref, ga_ref, cnt_ref, gb_ref, rk_ref, g_ref,
                        b_ref, o_ref, xb_ref, a_scr, w_scr, rk16_scr, gb16_scr, *, tm, ni):
    e = pl.program_id(1)

    @pl.when(e == 0)
    def _():
        o_ref[...] = jnp.zeros(o_ref.shape, F32)
        xb_ref[...] = res_ref[...].astype(BF16)

        def repack_head(h, carry):
            rk16_scr[h] = rk_ref[h]
            gb16_scr[h] = gb_ref[h]
            return carry

        lax.fori_loop(0, PEER_HEADS, repack_head, 0)

    group = 4
    slab = 16
    nslab = PEER_NKEYS // slab
    wide = 2 * LANES

    def routing_weight(a_rd, w_wr, col):
        for g0 in range(0, ni, group):
            accs = [[jnp.zeros((slab, LANES), BF16) for _ in range(nslab)] for _ in range(group)]
            for h in range(PEER_HEADS):
                rk = rk16_scr[h, :, col]
                gb = gb16_scr[h, :, col]
                rks = [rk[i * slab:(i + 1) * slab] for i in range(nslab)]
                gbs = [gb[i * slab:(i + 1) * slab] for i in range(nslab)]
                for k in range(group):
                    il = g0 + k
                    n16 = jnp.broadcast_to(cnt_ref[h, il:il + 1, col], (slab, LANES)).astype(BF16)
                    a16 = jnp.broadcast_to(ga_ref[h, il:il + 1, col], (slab, LANES)).astype(BF16)
                    for i in range(nslab):
                        accs[k][i] = accs[k][i] + jnp.where(rks[i] < n16, gbs[i] * a16,
                                                            jnp.zeros_like(a16))
            for k in range(group):
                r0 = (g0 + k) * PEER_NKEYS
                a = a_rd[r0:r0 + PEER_NKEYS, col].astype(BF16)
                gelu = (0.5 * a) * (1.0 + lax.erf(a * INV_SQRT2))
                w_wr[r0:r0 + PEER_NKEYS, col] = jnp.concatenate(accs[k], axis=0) * gelu

    a_scr[...] = jnp.dot(u_ref[...], xb_ref[...], preferred_element_type=F32)

    def vpu_pass(cc, carry):
        routing_weight(a_scr, w_scr, pl.ds(pl.multiple_of(cc * LANES, LANES), LANES))
        return carry

    lax.fori_loop(0, tm // LANES, vpu_pass, 0)
    o_ref[...] += jnp.dot(vt_ref[...], w_scr[...], preferred_element_type=F32)

    @pl.when(e == pl.num_programs(1) - 1)
    def _():
        def ln_cols(cc, carry):
            col = pl.ds(pl.multiple_of(cc * LANES, LANES), LANES)
            z = DN_ALPHA * res_ref[:, col] + o_ref[:, col]
            mu = jnp.mean(z, axis=0, keepdims=True)
            zc = z - mu
            var = jnp.mean(zc * zc, axis=0, keepdims=True)
            o_ref[:, col] = zc * lax.rsqrt(var + LN_EPS) * g_ref[...] + b_ref[...]
            return carry

        lax.fori_loop(0, tm // LANES, ln_cols, 0)


def _peer_experts(ht, u_bf16, vt_bf16, ga, cnt, gb, rk, g, b, *, tm=1024, eb=1024):
    D, T = ht.shape
    E = u_bf16.shape[0]
    tm = min(tm, T)
    ni = eb // PEER_NKEYS
    kern = functools.partial(_peer_expert_kernel, tm=tm, ni=ni)
    return pl.pallas_call(
        kern,
        grid=(T // tm, E // eb),
        in_specs=[
            pl.BlockSpec((D, tm), lambda t, e: (0, t)),
            pl.BlockSpec((eb, D), lambda t, e: (e, 0)),
            pl.BlockSpec((D, eb), lambda t, e: (0, e)),
            pl.BlockSpec((PEER_HEADS, ni, tm), lambda t, e: (0, e, t)),
            pl.BlockSpec((PEER_HEADS, ni, tm), lambda t, e: (0, e, t)),
            pl.BlockSpec((PEER_HEADS, PEER_NKEYS, tm), lambda t, e: (0, 0, t)),
            pl.BlockSpec((PEER_HEADS, PEER_NKEYS, tm), lambda t, e: (0, 0, t)),
            pl.BlockSpec((D, 1), lambda t, e: (0, 0)),
            pl.BlockSpec((D, 1), lambda t, e: (0, 0)),
        ],
        out_specs=pl.BlockSpec((D, tm), lambda t, e: (0, t)),
        out_shape=jax.ShapeDtypeStruct((D, T), F32),
        scratch_shapes=[pltpu.VMEM((D, tm), BF16),
                        pltpu.VMEM((eb, tm), F32), pltpu.VMEM((eb, tm), BF16),
                        pltpu.VMEM((PEER_HEADS, PEER_NKEYS, tm), BF16),
                        pltpu.VMEM((PEER_HEADS, PEER_NKEYS, tm), BF16)],
        compiler_params=pltpu.CompilerParams(
            dimension_semantics=("parallel", "arbitrary"), vmem_limit_bytes=VMEM_LIMIT),
        name="peer_experts",
    )(ht, u_bf16, vt_bf16, ga, cnt, gb, rk, g.reshape(D, 1), b.reshape(D, 1))


def _peer_layer(h, w_pq, sub_keys, u_tab, v_tab, g, b):
    ht = h.T
    wq_t = w_pq.T.astype(BF16)
    sk = sub_keys.reshape(2 * PEER_HEADS, PEER_NKEYS, PEER_HALF).astype(BF16)
    ga, cnt, gb, rk = _peer_route(ht, wq_t, sk)
    out_t = _peer_experts(ht, u_tab.astype(BF16), v_tab.T.astype(BF16), ga, cnt, gb, rk, g, b)
    return out_t.T


def _rope_lane_tables():
    d = jnp.arange(LANES) % SW_HEAD_DIM
    half = ROPE_DIMS // 2
    inv = ROPE_THETA ** (-(2.0 * (d % half).astype(F32)) / ROPE_DIMS)
    inv = jnp.where(d < ROPE_DIMS, inv, 0.0)
    sgn_up = jnp.where(d < half, -1.0, 0.0)
    sgn_dn = jnp.where((d >= half) & (d < ROPE_DIMS), 1.0, 0.0)
    return jnp.stack([inv, sgn_up, sgn_dn]).astype(F32)


def _rope_apply(x, pos_f, tab):
    n, width = x.shape
    reps = width // LANES
    ang = pos_f * tab[0:1, :]
    cos = jnp.tile(jnp.cos(ang), (1, reps))
    sin = jnp.tile(jnp.sin(ang), (1, reps))
    s_up = jnp.tile(tab[1:2, :], (1, reps))
    s_dn = jnp.tile(tab[2:3, :], (1, reps))
    half = ROPE_DIMS // 2
    partner = pltpu.roll(x, width - half, axis=1) * s_up + pltpu.roll(x, half, axis=1) * s_dn
    return x * cos + partner * sin


def _kv_kernel(h_ref, w_ref, pos_ref, tab_ref, k_ref, v_ref):
    kv = jnp.dot(h_ref[...].astype(BF16), w_ref[...], preferred_element_type=F32)
    nk2 = SW_KV_HEADS * LANES
    k_ref[...] = _rope_apply(kv[:, :nk2], pos_ref[...].astype(F32), tab_ref[...]).astype(BF16)
    v_ref[...] = kv[:, nk2:].astype(BF16)


def _dup_heads(w):
    D, n = w.shape
    w3 = w.reshape(D, n // SW_HEAD_DIM, SW_HEAD_DIM)
    return jnp.concatenate([w3, w3], axis=2).reshape(D, 2 * n)


def _shared_kv(h, w_kv, pos, tab, *, tm=512):
    T, D = h.shape
    tm = min(tm, T)
    nk = SW_KV_HEADS * SW_HEAD_DIM
    nk2 = SW_KV_HEADS * LANES
    w_kv_bf16 = jnp.concatenate([_dup_heads(w_kv[:, :nk]), _dup_heads(w_kv[:, nk:])],
                                axis=1).astype(BF16)
    return pl.pallas_call(
        _kv_kernel,
        grid=(T // tm,),
        in_specs=[
            pl.BlockSpec((tm, D), lambda i: (i, 0)),
            pl.BlockSpec(w_kv_bf16.shape, lambda i: (0, 0)),
            pl.BlockSpec((tm, 1), lambda i: (i, 0)),
            pl.BlockSpec((3, LANES), lambda i: (0, 0)),
        ],
        out_specs=[pl.BlockSpec((tm, nk2), lambda i: (i, 0)),
                   pl.BlockSpec((tm, nk2), lambda i: (i, 0))],
        out_shape=[jax.ShapeDtypeStruct((T, nk2), BF16), jax.ShapeDtypeStruct((T, nk2), BF16)],
        compiler_params=pltpu.CompilerParams(
            dimension_semantics=("parallel",), vmem_limit_bytes=VMEM_LIMIT),
        name="shared_kv",
    )(h, w_kv_bf16, pos, tab)


def _swa_kernel(sink_ref, h_ref, wq_ref, k_ref, v_ref, pos_ref, tab_ref, wo_ref, g_ref, b_ref,
                o_ref, q_scr, att_scr, *, tq, seq):
    j = pl.program_id(1)
    W = SW_WINDOW
    hq = h_ref[...]
    q = jnp.dot(hq.astype(BF16), wq_ref[...], preferred_element_type=F32)
    q = _rope_apply(q, pos_ref[...].astype(F32), tab_ref[...]) * (SW_HEAD_DIM ** -0.5)
    q_scr[...] = q.astype(BF16)

    qi = lax.broadcasted_iota(jnp.int32, (W, 2 * W), 0)
    kj = lax.broadcasted_iota(jnp.int32, (W, 2 * W), 1)
    lane = lax.broadcasted_iota(jnp.int32, (W, LANES), 1)
    first_half = lane < SW_HEAD_DIM
    group = SW_Q_HEADS // SW_KV_HEADS

    def block(n, carry):
        q0 = j * tq + n * W
        k0 = jnp.maximum(q0 - W, 0)
        rows = pl.ds(pl.multiple_of(n * W, W), W)
        band = pl.ds(pl.multiple_of(k0, W), 2 * W)
        diff = (q0 + qi) - (k0 + kj)
        valid = (diff >= 0) & (diff < W)
        for pair in range(SW_Q_HEADS // 2):
            kvh = (2 * pair) // group
            lanes = slice(pair * LANES, (pair + 1) * LANES)
            qp = q_scr[rows, lanes]
            kb = k_ref[0, band, kvh * LANES:(kvh + 1) * LANES]
            vb = v_ref[0, band, kvh * LANES:(kvh + 1) * LANES]
            outs = []
            for sub in range(2):
                head = 2 * pair + sub
                keep = first_half if sub == 0 else jnp.logical_not(first_half)
                qh = jnp.where(keep, qp, jnp.zeros_like(qp))
                logits = jnp.where(valid, _nt_dot(qh, kb), NEG_INF)
                sink = sink_ref[head]
                m = jnp.maximum(jnp.max(logits, axis=-1, keepdims=True), sink)
                p = jnp.exp(logits - m)
                denom = jnp.sum(p, axis=-1, keepdims=True) + jnp.exp(sink - m)
                p = p * (1.0 / denom)
                outs.append(jnp.dot(p.astype(BF16), vb, preferred_element_type=F32))
            att_scr[rows, lanes] = jnp.where(first_half, outs[0], outs[1]).astype(BF16)
        return carry

    lax.fori_loop(0, tq // W, block, 0)
    y = jnp.dot(att_scr[...], wo_ref[...], preferred_element_type=F32)
    o_ref[...] = _layernorm_rows(DN_ALPHA * hq + y, g_ref[...], b_ref[...])


def _swa_layer(h, k_dup, v_dup, pos, tab, w_q_bf16, sinks, w_o_bf16, g, b, *, batch, tq=512):
    T, D = h.shape
    S = T // batch
    tq = min(tq, S)
    nq = S // tq
    nk2 = k_dup.shape[1]
    k3 = k_dup.reshape(batch, S, nk2)
    v3 = v_dup.reshape(batch, S, nk2)
    kern = functools.partial(_swa_kernel, tq=tq, seq=S)
    return pl.pallas_call(
        kern,
        grid=(batch, nq),
        in_specs=[
            pl.BlockSpec(memory_space=pltpu.SMEM),
            pl.BlockSpec((tq, D), lambda bb, j: (bb * nq + j, 0)),
            pl.BlockSpec((D, D), lambda bb, j: (0, 0)),
            pl.BlockSpec((1, S, nk2), lambda bb, j: (bb, 0, 0)),
            pl.BlockSpec((1, S, nk2), lambda bb, j: (bb, 0, 0)),
            pl.BlockSpec((tq, 1), lambda bb, j: (bb * nq + j, 0)),
            pl.BlockSpec((3, LANES), lambda bb, j: (0, 0)),
            pl.BlockSpec((D, D), lambda bb, j: (0, 0)),
            pl.BlockSpec((1, D), lambda bb, j: (0, 0)),
            pl.BlockSpec((1, D), lambda bb, j: (0, 0)),
        ],
        out_specs=pl.BlockSpec((tq, D), lambda bb, j: (bb * nq + j, 0)),
        out_shape=jax.ShapeDtypeStruct((T, D), F32),
        scratch_shapes=[pltpu.VMEM((tq, D), BF16), pltpu.VMEM((tq, D), BF16)],
        compiler_params=pltpu.CompilerParams(
            dimension_semantics=("parallel", "arbitrary"), vmem_limit_bytes=VMEM_LIMIT),
        name="swa_layer",
    )(sinks, h, w_q_bf16, k3, v3, pos, tab, w_o_bf16, g.reshape(1, D), b.reshape(1, D))


def kernel(x, positions, hg_w_in, hg_lb, hg_norm_w, hg_w_o, sw_w_q, sw_sinks, sw_w_o, w_kv,
           peer_w_q, peer_sub_keys, peer_u, peer_v, ln_mix_g, ln_mix_b, ln_ffn_g, ln_ffn_b):
    B, S, D = x.shape
    T = B * S
    H = HG_HEADS
    pos = positions.reshape(T, 1)
    tab = _rope_lane_tables()

    lb_all = jnp.cumsum(jax.nn.softmax(hg_lb.astype(F32), axis=0), axis=0)
    w_heads = (hg_w_in[0].reshape(D, 4, H, HG_DK).transpose(2, 0, 1, 3)
               .reshape(H, D, 4 * HG_DK).astype(BF16))
    mix = _hgrn2_mix(x.astype(BF16), w_heads, lb_all[0].reshape(H, 1, HG_DK),
                     hg_norm_w[0].reshape(H, 1, HG_DV))
    h = _proj_res_ln(mix.reshape(T, H * HG_DV), hg_w_o[0].astype(BF16), x.reshape(T, D),
                     ln_mix_g[0], ln_mix_b[0])
    h = _peer_layer(h, peer_w_q[0], peer_sub_keys[0], peer_u[0], peer_v[0],
                    ln_ffn_g[0], ln_ffn_b[0])

    k_dup, v_dup = _shared_kv(h, w_kv, pos, tab)
    h = _swa_layer(h, k_dup, v_dup, pos, tab, sw_w_q[0].astype(BF16), sw_sinks[0].astype(F32),
                   sw_w_o[0].astype(BF16), ln_mix_g[1], ln_mix_b[1], batch=B)
    h = _peer_layer(h, peer_w_q[1], peer_sub_keys[1], peer_u[1], peer_v[1],
                    ln_ffn_g[1], ln_ffn_b[1])
    return h.reshape(B, S, D)
```

```python
import functools
import math

import jax
import jax.numpy as jnp
from jax import lax
from jax.experimental import pallas as pl
from jax.experimental.pallas import tpu as pltpu

F32 = jnp.float32
BF16 = jnp.bfloat16

D_MODEL = 1024
DEPTH = 2
HG_HEADS = 8
HG_DK = 128
HG_DV = 128
HG_CHUNK = 32
SW_Q_HEADS = 16
SW_KV_HEADS = 4
SW_HEAD_DIM = 64
SW_WINDOW = 128
ROPE_THETA = 500000.0
ROPE_DIMS = 16
PEER_HEADS = 8
PEER_NKEYS = 128
PEER_HALF = 128
PEER_TOPK = 16
DN_ALPHA = (2.0 * DEPTH) ** 0.25
LN_EPS = 1e-5
RMS_EPS = 1e-6

LANES = 128
VMEM_LIMIT = 56 * 1024 * 1024
NEG_INF = float("-inf")
INV_SQRT2 = 1.0 / math.sqrt(2.0)


def _sigmoid(x):
    return 1.0 / (1.0 + jnp.exp(-x))


def _nt_dot(a, b):
    return lax.dot_general(a, b, (((1,), (1,)), ((), ())), preferred_element_type=F32)


def _tn_dot(a, b):
    return lax.dot_general(a, b, (((0,), (0,)), ((), ())), preferred_element_type=F32)


def _hgrn2_kernel(x_ref, w_ref, lb_ref, nw_ref, o_ref, proj_ref, *, seq, chunk, row_tile):
    def project(i, carry):
        r = pl.ds(pl.multiple_of(i * row_tile, row_tile), row_tile)
        proj_ref[r, 0:4 * HG_DK] = jnp.dot(x_ref[0, r, :], w_ref[0], preferred_element_type=F32)
        return carry

    lax.fori_loop(0, seq // row_tile, project, 0)

    lb = lb_ref[0]
    nw = nw_ref[0]
    row = lax.broadcasted_iota(jnp.int32, (chunk, HG_DK), 0)
    causal = (lax.broadcasted_iota(jnp.int32, (chunk, chunk), 1)
              <= lax.broadcasted_iota(jnp.int32, (chunk, chunk), 0))

    def step(c, state_t):
        r = pl.ds(pl.multiple_of(c * chunk, chunk), chunk)
        q = proj_ref[r, 0:128]
        f = proj_ref[r, 128:256]
        v = proj_ref[r, 256:384]
        g = proj_ref[r, 384:512]
        q = q * _sigmoid(q)
        fg = lb + (1.0 - lb) * _sigmoid(f)
        k = 1.0 - fg
        b = jnp.log(fg)
        s = 1
        while s < chunk:
            b = b + jnp.where(row >= s, pltpu.roll(b, s, axis=0), 0.0)
            s *= 2
        b_last = b[chunk - 1:chunk, :]
        qd = (q * jnp.exp(b)).astype(BF16)
        kd = (k * jnp.exp(-b)).astype(BF16)
        kl = (k * jnp.exp(b_last - b)).astype(BF16)
        vb = v.astype(BF16)
        att = jnp.where(causal, _nt_dot(qd, kd), 0.0)
        o = (jnp.dot(att.astype(BF16), vb, preferred_element_type=F32)
             + _nt_dot(qd, state_t.astype(BF16)))
        state_t = state_t * jnp.exp(b_last) + _tn_dot(vb, kl)
        o = o * lax.rsqrt(jnp.mean(o * o, axis=-1, keepdims=True) + RMS_EPS)
        o_ref[0, r, :] = o * nw * (g * _sigmoid(g))
        return state_t

    lax.fori_loop(0, seq // chunk, step, jnp.zeros((HG_DV, HG_DK), F32), unroll=2)


def _hgrn2_mix(xb, w_heads, lb, norm_w):
    B, S, D = xb.shape
    H = w_heads.shape[0]
    kern = functools.partial(_hgrn2_kernel, seq=S, chunk=HG_CHUNK, row_tile=min(512, S))
    return pl.pallas_call(
        kern,
        grid=(B, H),
        in_specs=[
            pl.BlockSpec((1, S, D), lambda b, h: (b, 0, 0)),
            pl.BlockSpec((1, D, 4 * HG_DK), lambda b, h: (h, 0, 0)),
            pl.BlockSpec((1, 1, HG_DK), lambda b, h: (h, 0, 0)),
            pl.BlockSpec((1, 1, HG_DV), lambda b, h: (h, 0, 0)),
        ],
        out_specs=pl.BlockSpec((1, S, HG_DV), lambda b, h: (b, 0, h)),
        out_shape=jax.ShapeDtypeStruct((B, S, H * HG_DV), F32),
        scratch_shapes=[pltpu.VMEM((S, 4 * HG_DK + LANES), F32)],
        compiler_params=pltpu.CompilerParams(
            dimension_semantics=("parallel", "arbitrary"), vmem_limit_bytes=VMEM_LIMIT),
        name="hgrn2_mix",
    )(xb, w_heads, lb, norm_w)


def _layernorm_rows(z, g, b):
    mu = jnp.mean(z, axis=-1, keepdims=True)
    zc = z - mu
    var = jnp.mean(zc * zc, axis=-1, keepdims=True)
    return zc * lax.rsqrt(var + LN_EPS) * g + b


def _proj_res_ln_kernel(a_ref, w_ref, res_ref, g_ref, b_ref, o_ref):
    y = jnp.dot(a_ref[...].astype(BF16), w_ref[...], preferred_element_type=F32)
    o_ref[...] = _layernorm_rows(DN_ALPHA * res_ref[...] + y, g_ref[...], b_ref[...])


def _proj_res_ln(a, w_bf16, res, g, b, *, tm=512):
    T, K = a.shape
    N = w_bf16.shape[1]
    tm = min(tm, T)
    return pl.pallas_call(
        _proj_res_ln_kernel,
        grid=(T // tm,),
        in_specs=[
            pl.BlockSpec((tm, K), lambda i: (i, 0)),
            pl.BlockSpec((K, N), lambda i: (0, 0)),
            pl.BlockSpec((tm, N), lambda i: (i, 0)),
            pl.BlockSpec((1, N), lambda i: (0, 0)),
            pl.BlockSpec((1, N), lambda i: (0, 0)),
        ],
        out_specs=pl.BlockSpec((tm, N), lambda i: (i, 0)),
        out_shape=jax.ShapeDtypeStruct((T, N), F32),
        compiler_params=pltpu.CompilerParams(
            dimension_semantics=("parallel",), vmem_limit_bytes=VMEM_LIMIT),
        name="proj_res_ln",
    )(a, w_bf16, res, g.reshape(1, N), b.reshape(1, N))


def _top16_desc(s):
    vals = []
    rank = jnp.full(s.shape, 255.0, F32)
    for q in range(PEER_TOPK):
        m = jnp.max(s, axis=0, keepdims=True)
        hit = s == m
        rank = jnp.where(hit, float(q), rank)
        s = jnp.where(hit, NEG_INF, s)
        vals.append(m)
    return vals, rank


def _rows_to_slab(vals, lo):
    n = vals[0].shape[1]
    rid = lax.broadcasted_iota(jnp.int32, (8, n), 0)
    slab = jnp.zeros((8, n), F32)
    for r in range(8):
        slab = jnp.where(rid == r, vals[lo + r], slab)
    return slab


def _peer_route_kernel(ht_ref, wq_ref, sk_ref, ga_ref, cnt_ref, gb_ref, rk_ref, q_scr, s_scr, *, tm):
    hb = ht_ref[...].astype(BF16)
    q_scr[...] = jnp.dot(wq_ref[...], hb, preferred_element_type=F32).astype(BF16)
    for hc in range(2 * PEER_HEADS):
        s_scr[hc, :, 0:tm] = jnp.dot(sk_ref[hc], q_scr[hc * PEER_HALF:(hc + 1) * PEER_HALF, :],
                                     preferred_element_type=F32)

    ncol = tm // LANES
    rid8 = lax.broadcasted_iota(jnp.int32, (8, LANES), 0)

    def body(it, carry):
        hh = it // ncol
        cc = it % ncol
        col = pl.ds(pl.multiple_of(cc * LANES, LANES), LANES)
        s1 = s_scr[2 * hh, :, col]
        s2 = s_scr[2 * hh + 1, :, col]
        a_vals, _ = _top16_desc(s1)
        b_vals, rank2 = _top16_desc(s2)
        a_lo = _rows_to_slab(a_vals, 0)
        a_hi = _rows_to_slab(a_vals, 8)
        slabs = [a_lo + b_vals[0], a_hi + b_vals[0]]
        for q in range(1, PEER_TOPK):
            keep = PEER_TOPK // (q + 1)
            slabs.append(jnp.where(rid8 < keep, a_lo + b_vals[q], NEG_INF))
        cand = jnp.concatenate(slabs, axis=0)
        for _ in range(PEER_TOPK - 1):
            m = jnp.max(cand, axis=0, keepdims=True)
            cand = jnp.where(cand == m, NEG_INF, cand)
        thr = jnp.max(cand, axis=0, keepdims=True)

        ea_lo = jnp.exp(a_lo - a_vals[0])
        ea_hi = jnp.exp(a_hi - a_vals[0])
        cnt = jnp.zeros(s1.shape, F32)
        zacc = jnp.zeros((8, LANES), F32)
        for q in range(PEER_TOPK):
            bq = b_vals[q]
            cnt = cnt + jnp.where(s1 + bq >= thr, 1.0, 0.0)
            part = (jnp.where(a_lo + bq >= thr, ea_lo, 0.0)
                    + jnp.where(a_hi + bq >= thr, ea_hi, 0.0))
            zacc = zacc + jnp.exp(bq - b_vals[0]) * part
        inv_z = 1.0 / jnp.sum(zacc, axis=0, keepdims=True)

        ga_ref[hh, cc] = jnp.exp(s1 - a_vals[0])
        cnt_ref[hh, cc] = cnt
        gb_ref[hh, cc] = (jnp.exp(s2 - b_vals[0]) * inv_z).astype(BF16)
        rk_ref[hh, cc] = rank2.astype(BF16)
        return carry

    lax.fori_loop(0, PEER_HEADS * ncol, body, 0)


def _peer_route(ht, wq_t, sub_keys, *, tm=512):
    D, T = ht.shape
    tm = min(tm, T)
    nq = wq_t.shape[0]
    kern = functools.partial(_peer_route_kernel, tm=tm)
    head_rows = pl.BlockSpec((PEER_HEADS, tm // LANES, PEER_NKEYS, LANES), lambda i: (0, i, 0, 0))
    shp = (PEER_HEADS, T // LANES, PEER_NKEYS, LANES)
    return pl.pallas_call(
        kern,
        grid=(T // tm,),
        in_specs=[
            pl.BlockSpec((D, tm), lambda i: (0, i)),
            pl.BlockSpec((nq, D), lambda i: (0, 0)),
            pl.BlockSpec((2 * PEER_HEADS, PEER_NKEYS, PEER_HALF), lambda i: (0, 0, 0)),
        ],
        out_specs=[head_rows, head_rows, head_rows, head_rows],
        out_shape=[jax.ShapeDtypeStruct(shp, F32), jax.ShapeDtypeStruct(shp, F32),
                   jax.ShapeDtypeStruct(shp, BF16), jax.ShapeDtypeStruct(shp, BF16)],
        scratch_shapes=[pltpu.VMEM((nq, tm), BF16),
                        pltpu.VMEM((2 * PEER_HEADS, PEER_NKEYS, tm + LANES), F32)],
        compiler_params=pltpu.CompilerParams(
            dimension_semantics=("parallel",), vmem_limit_bytes=VMEM_LIMIT),
        name="peer_route",
    )(ht, wq_t, sub_keys)


def _peer_expert_kernel(res_ref, u_ref, vt_ref, ga_ref, cnt_ref, gb_ref, rk_ref, g_ref,
                        b_ref, o_ref, xb_ref, a_scr, w_scr, rk16_scr, gb16_scr, *, tm, ni):
    e = pl.program_id(1)

    @pl.when(e == 0)
    def _():
        o_ref[...] = jnp.zeros(o_ref.shape, F32)
        xb_ref[...] = res_ref[...].astype(BF16)

        def repack_head(h, carry):
            for c in range(tm // LANES):
                rk16_scr[h, :, c * LANES:(c + 1) * LANES] = rk_ref[h, c]
                gb16_scr[h, :, c * LANES:(c + 1) * LANES] = gb_ref[h, c]
            return carry

        lax.fori_loop(0, PEER_HEADS, repack_head, 0)

    group = 4
    slab = 16
    nslab = PEER_NKEYS // slab
    wide = 2 * LANES

    def routing_weight(a_rd, w_wr, cc):
        col = pl.ds(pl.multiple_of(cc * LANES, LANES), LANES)
        for g0 in range(0, ni, group):
            accs = [[jnp.zeros((slab, LANES), BF16) for _ in range(nslab)] for _ in range(group)]
            for h in range(PEER_HEADS):
                rk = rk16_scr[h, :, col]
                gb = gb16_scr[h, :, col]
                rks = [rk[i * slab:(i + 1) * slab] for i in range(nslab)]
                gbs = [gb[i * slab:(i + 1) * slab] for i in range(nslab)]
                for k in range(group):
                    il = g0 + k
                    n16 = jnp.broadcast_to(cnt_ref[h, cc, il:il + 1, :], (slab, LANES)).astype(BF16)
                    a16 = jnp.broadcast_to(ga_ref[h, cc, il:il + 1, :], (slab, LANES)).astype(BF16)
                    for i in range(nslab):
                        accs[k][i] = accs[k][i] + jnp.where(rks[i] < n16, gbs[i] * a16,
                                                            jnp.zeros_like(a16))
            for k in range(group):
                r0 = (g0 + k) * PEER_NKEYS
                a = a_rd[r0:r0 + PEER_NKEYS, col].astype(BF16)
                gelu = (0.5 * a) * (1.0 + lax.erf(a * INV_SQRT2))
                w_wr[r0:r0 + PEER_NKEYS, col] = jnp.concatenate(accs[k], axis=0) * gelu

    a_scr[:, 0:tm] = jnp.dot(u_ref[...], xb_ref[...], preferred_element_type=F32)

    def vpu_pass(cc, carry):
        routing_weight(a_scr, w_scr, cc)
        return carry

    lax.fori_loop(0, tm // LANES, vpu_pass, 0)
    o_ref[...] += jnp.dot(vt_ref[...], w_scr[:, 0:tm], preferred_element_type=F32)

    @pl.when(e == pl.num_programs(1) - 1)
    def _():
        n_blk = o_ref.shape[0] // 8
        inv_d = 1.0 / o_ref.shape[0]

        def rows(i):
            return pl.ds(pl.multiple_of(i * 8, 8), 8)

        def add_z(i, acc):
            z = DN_ALPHA * res_ref[rows(i), :] + o_ref[rows(i), :]
            o_ref[rows(i), :] = z
            return acc + z

        zero = jnp.zeros((8, tm), F32)
        mu = jnp.sum(lax.fori_loop(0, n_blk, add_z, zero), axis=0, keepdims=True) * inv_d

        def add_sq(i, acc):
            zc = o_ref[rows(i), :] - mu
            return acc + zc * zc

        var = jnp.sum(lax.fori_loop(0, n_blk, add_sq, zero), axis=0, keepdims=True) * inv_d
        rstd = lax.rsqrt(var + LN_EPS)

        def normalize(i, carry):
            zc = o_ref[rows(i), :] - mu
            o_ref[rows(i), :] = zc * rstd * g_ref[rows(i), :] + b_ref[rows(i), :]
            return carry

        lax.fori_loop(0, n_blk, normalize, 0)


def _peer_experts(ht, u_bf16, vt_bf16, ga, cnt, gb, rk, g, b, *, tm=1024, eb=1024):
    D, T = ht.shape
    E = u_bf16.shape[0]
    tm = min(tm, T)
    ni = eb // PEER_NKEYS
    kern = functools.partial(_peer_expert_kernel, tm=tm, ni=ni)
    return pl.pallas_call(
        kern,
        grid=(T // tm, E // eb),
        in_specs=[
            pl.BlockSpec((D, tm), lambda t, e: (0, t)),
            pl.BlockSpec((eb, D), lambda t, e: (e, 0)),
            pl.BlockSpec((D, eb), lambda t, e: (0, e)),
            pl.BlockSpec((PEER_HEADS, tm // LANES, ni, LANES), lambda t, e: (0, t, e, 0)),
            pl.BlockSpec((PEER_HEADS, tm // LANES, ni, LANES), lambda t, e: (0, t, e, 0)),
            pl.BlockSpec((PEER_HEADS, tm // LANES, PEER_NKEYS, LANES), lambda t, e: (0, t, 0, 0)),
            pl.BlockSpec((PEER_HEADS, tm // LANES, PEER_NKEYS, LANES), lambda t, e: (0, t, 0, 0)),
            pl.BlockSpec((D, 1), lambda t, e: (0, 0)),
            pl.BlockSpec((D, 1), lambda t, e: (0, 0)),
        ],
        out_specs=pl.BlockSpec((D, tm), lambda t, e: (0, t)),
        out_shape=jax.ShapeDtypeStruct((D, T), F32),
        scratch_shapes=[pltpu.VMEM((D, tm), BF16),
                        pltpu.VMEM((eb, tm + LANES), F32), pltpu.VMEM((eb, tm + LANES), BF16),
                        pltpu.VMEM((PEER_HEADS, PEER_NKEYS, tm + LANES), BF16),
                        pltpu.VMEM((PEER_HEADS, PEER_NKEYS, tm + LANES), BF16)],
        compiler_params=pltpu.CompilerParams(
            dimension_semantics=("parallel", "arbitrary"), vmem_limit_bytes=VMEM_LIMIT),
        name="peer_experts",
    )(ht, u_bf16, vt_bf16, ga, cnt, gb, rk, g.reshape(D, 1), b.reshape(D, 1))


def _peer_layer(h, w_pq, sub_keys, u_tab, v_tab, g, b):
    ht = h.T
    wq_t = w_pq.T.astype(BF16)
    sk = sub_keys.reshape(2 * PEER_HEADS, PEER_NKEYS, PEER_HALF).astype(BF16)
    ga, cnt, gb, rk = _peer_route(ht, wq_t, sk)
    out_t = _peer_experts(ht, u_tab.astype(BF16), v_tab.T.astype(BF16), ga, cnt, gb, rk, g, b)
    return out_t.T


def _rope_lane_tables():
    d = jnp.arange(LANES) % SW_HEAD_DIM
    half = ROPE_DIMS // 2
    inv = ROPE_THETA ** (-(2.0 * (d % half).astype(F32)) / ROPE_DIMS)
    inv = jnp.where(d < ROPE_DIMS, inv, 0.0)
    sgn_up = jnp.where(d < half, -1.0, 0.0)
    sgn_dn = jnp.where((d >= half) & (d < ROPE_DIMS), 1.0, 0.0)
    return jnp.stack([inv, sgn_up, sgn_dn]).astype(F32)


def _rope_apply(x, pos_f, tab):
    n, width = x.shape
    reps = width // LANES
    ang = pos_f * tab[0:1, :]
    cos = jnp.tile(jnp.cos(ang), (1, reps))
    sin = jnp.tile(jnp.sin(ang), (1, reps))
    s_up = jnp.tile(tab[1:2, :], (1, reps))
    s_dn = jnp.tile(tab[2:3, :], (1, reps))
    half = ROPE_DIMS // 2
    partner = pltpu.roll(x, width - half, axis=1) * s_up + pltpu.roll(x, half, axis=1) * s_dn
    return x * cos + partner * sin


def _kv_kernel(h_ref, w_ref, pos_ref, tab_ref, k_ref, v_ref):
    kv = jnp.dot(h_ref[...].astype(BF16), w_ref[...], preferred_element_type=F32)
    nk2 = SW_KV_HEADS * LANES
    k_ref[...] = _rope_apply(kv[:, :nk2], pos_ref[...].astype(F32), tab_ref[...]).astype(BF16)
    v_ref[...] = kv[:, nk2:].astype(BF16)


def _dup_heads(w):
    D, n = w.shape
    w3 = w.reshape(D, n // SW_HEAD_DIM, SW_HEAD_DIM)
    return jnp.concatenate([w3, w3], axis=2).reshape(D, 2 * n)


def _shared_kv(h, w_kv, pos, tab, *, tm=512):
    T, D = h.shape
    tm = min(tm, T)
    nk = SW_KV_HEADS * SW_HEAD_DIM
    nk2 = SW_KV_HEADS * LANES
    w_kv_bf16 = jnp.concatenate([_dup_heads(w_kv[:, :nk]), _dup_heads(w_kv[:, nk:])],
                                axis=1).astype(BF16)
    return pl.pallas_call(
        _kv_kernel,
        grid=(T // tm,),
        in_specs=[
            pl.BlockSpec((tm, D), lambda i: (i, 0)),
            pl.BlockSpec(w_kv_bf16.shape, lambda i: (0, 0)),
            pl.BlockSpec((tm, 1), lambda i: (i, 0)),
            pl.BlockSpec((3, LANES), lambda i: (0, 0)),
        ],
        out_specs=[pl.BlockSpec((tm, nk2), lambda i: (i, 0)),
                   pl.BlockSpec((tm, nk2), lambda i: (i, 0))],
        out_shape=[jax.ShapeDtypeStruct((T, nk2), BF16), jax.ShapeDtypeStruct((T, nk2), BF16)],
        compiler_params=pltpu.CompilerParams(
            dimension_semantics=("parallel",), vmem_limit_bytes=VMEM_LIMIT),
        name="shared_kv",
    )(h, w_kv_bf16, pos, tab)


def _swa_kernel(sink_ref, h_ref, wq_ref, k_ref, v_ref, pos_ref, tab_ref, wo_ref, g_ref, b_ref,
                o_ref, q_scr, att_scr, *, tq, seq):
    j = pl.program_id(1)
    W = SW_WINDOW
    hq = h_ref[...]
    q = jnp.dot(hq.astype(BF16), wq_ref[...], preferred_element_type=F32)
    q = _rope_apply(q, pos_ref[...].astype(F32), tab_ref[...]) * (SW_HEAD_DIM ** -0.5)
    q_scr[...] = q.astype(BF16)

    qi = lax.broadcasted_iota(jnp.int32, (W, 2 * W), 0)
    kj = lax.broadcasted_iota(jnp.int32, (W, 2 * W), 1)
    lane = lax.broadcasted_iota(jnp.int32, (W, LANES), 1)
    first_half = lane < SW_HEAD_DIM
    group = SW_Q_HEADS // SW_KV_HEADS

    def block(n, carry):
        q0 = j * tq + n * W
        k0 = jnp.maximum(q0 - W, 0)
        rows = pl.ds(pl.multiple_of(n * W, W), W)
        band = pl.ds(pl.multiple_of(k0, W), 2 * W)
        diff = (q0 + qi) - (k0 + kj)
        valid = (diff >= 0) & (diff < W)
        for pair in range(SW_Q_HEADS // 2):
            kvh = (2 * pair) // group
            lanes = slice(pair * LANES, (pair + 1) * LANES)
            qp = q_scr[rows, lanes]
            kb = k_ref[0, band, kvh * LANES:(kvh + 1) * LANES]
            vb = v_ref[0, band, kvh * LANES:(kvh + 1) * LANES]
            outs = []
            for sub in range(2):
                head = 2 * pair + sub
                keep = first_half if sub == 0 else jnp.logical_not(first_half)
                qh = jnp.where(keep, qp, jnp.zeros_like(qp))
                logits = jnp.where(valid, _nt_dot(qh, kb), NEG_INF)
                sink = sink_ref[head]
                m = jnp.maximum(jnp.max(logits, axis=-1, keepdims=True), sink)
                p = jnp.exp(logits - m)
                denom = jnp.sum(p, axis=-1, keepdims=True) + jnp.exp(sink - m)
                p = p * (1.0 / denom)
                outs.append(jnp.dot(p.astype(BF16), vb, preferred_element_type=F32))
            att_scr[rows, lanes] = jnp.where(first_half, outs[0], outs[1]).astype(BF16)
        return carry

    lax.fori_loop(0, tq // W, block, 0)
    y = jnp.dot(att_scr[...], wo_ref[...], preferred_element_type=F32)
    o_ref[...] = _layernorm_rows(DN_ALPHA * hq + y, g_ref[...], b_ref[...])


def _swa_layer(h, k_dup, v_dup, pos, tab, w_q_bf16, sinks, w_o_bf16, g, b, *, batch, tq=512):
    T, D = h.shape
    S = T // batch
    tq = min(tq, S)
    nq = S // tq
    nk2 = k_dup.shape[1]
    k3 = k_dup.reshape(batch, S, nk2)
    v3 = v_dup.reshape(batch, S, nk2)
    kern = functools.partial(_swa_kernel, tq=tq, seq=S)
    return pl.pallas_call(
        kern,
        grid=(batch, nq),
        in_specs=[
            pl.BlockSpec(memory_space=pltpu.SMEM),
            pl.BlockSpec((tq, D), lambda bb, j: (bb * nq + j, 0)),
            pl.BlockSpec((D, D), lambda bb, j: (0, 0)),
            pl.BlockSpec((1, S, nk2), lambda bb, j: (bb, 0, 0)),
            pl.BlockSpec((1, S, nk2), lambda bb, j: (bb, 0, 0)),
            pl.BlockSpec((tq, 1), lambda bb, j: (bb * nq + j, 0)),
            pl.BlockSpec((3, LANES), lambda bb, j: (0, 0)),
            pl.BlockSpec((D, D), lambda bb, j: (0, 0)),
            pl.BlockSpec((1, D), lambda bb, j: (0, 0)),
            pl.BlockSpec((1, D), lambda bb, j: (0, 0)),
        ],
        out_specs=pl.BlockSpec((tq, D), lambda bb, j: (bb * nq + j, 0)),
        out_shape=jax.ShapeDtypeStruct((T, D), F32),
        scratch_shapes=[pltpu.VMEM((tq, D), BF16), pltpu.VMEM((tq, D), BF16)],
        compiler_params=pltpu.CompilerParams(
            dimension_semantics=("parallel", "arbitrary"), vmem_limit_bytes=VMEM_LIMIT),
        name="swa_layer",
    )(sinks, h, w_q_bf16, k3, v3, pos, tab, w_o_bf16, g.reshape(1, D), b.reshape(1, D))


def kernel(x, positions, hg_w_in, hg_lb, hg_norm_w, hg_w_o, sw_w_q, sw_sinks, sw_w_o, w_kv,
           peer_w_q, peer_sub_keys, peer_u, peer_v, ln_mix_g, ln_mix_b, ln_ffn_g, ln_ffn_b):
    B, S, D = x.shape
    T = B * S
    H = HG_HEADS
    pos = positions.reshape(T, 1)
    tab = _rope_lane_tables()

    lb_all = jnp.cumsum(jax.nn.softmax(hg_lb.astype(F32), axis=0), axis=0)
    w_heads = (hg_w_in[0].reshape(D, 4, H, HG_DK).transpose(2, 0, 1, 3)
               .reshape(H, D, 4 * HG_DK).astype(BF16))
    mix = _hgrn2_mix(x.astype(BF16), w_heads, lb_all[0].reshape(H, 1, HG_DK),
                     hg_norm_w[0].reshape(H, 1, HG_DV))
    h = _proj_res_ln(mix.reshape(T, H * HG_DV), hg_w_o[0].astype(BF16), x.reshape(T, D),
                     ln_mix_g[0], ln_mix_b[0])
    h = _peer_layer(h, peer_w_q[0], peer_sub_keys[0], peer_u[0], peer_v[0],
                    ln_ffn_g[0], ln_ffn_b[0])

    k_dup, v_dup = _shared_kv(h, w_kv, pos, tab)
    h = _swa_layer(h, k_dup, v_dup, pos, tab, sw_w_q[0].astype(BF16), sw_sinks[0].astype(F32),
                   sw_w_o[0].astype(BF16), ln_mix_g[1], ln_mix_b[1], batch=B)
    h = _peer_layer(h, peer_w_q[1], peer_sub_keys[1], peer_u[1], peer_v[1],
                    ln_ffn_g[1], ln_ffn_b[1])
    return h.reshape(B, S, D)
```

```python
import functools
import math

import jax
import jax.numpy as jnp
from jax import lax
from jax.experimental import pallas as pl
from jax.experimental.pallas import tpu as pltpu

F32 = jnp.float32
BF16 = jnp.bfloat16

D_MODEL = 1024
DEPTH = 2
HG_HEADS = 8
HG_DK = 128
HG_DV = 128
HG_CHUNK = 32
SW_Q_HEADS = 16
SW_KV_HEADS = 4
SW_HEAD_DIM = 64
SW_WINDOW = 128
ROPE_THETA = 500000.0
ROPE_DIMS = 16
PEER_HEADS = 8
PEER_NKEYS = 128
PEER_HALF = 128
PEER_TOPK = 16
DN_ALPHA = (2.0 * DEPTH) ** 0.25
LN_EPS = 1e-5
RMS_EPS = 1e-6

LANES = 128
VMEM_LIMIT = 56 * 1024 * 1024
NEG_INF = float("-inf")
INV_SQRT2 = 1.0 / math.sqrt(2.0)


def _sigmoid(x):
    return 1.0 / (1.0 + jnp.exp(-x))


def _nt_dot(a, b):
    return lax.dot_general(a, b, (((1,), (1,)), ((), ())), preferred_element_type=F32)


def _tn_dot(a, b):
    return lax.dot_general(a, b, (((0,), (0,)), ((), ())), preferred_element_type=F32)


def _hgrn2_kernel(x_ref, w_ref, lb_ref, nw_ref, o_ref, proj_ref, *, seq, chunk, row_tile):
    def project(i, carry):
        r = pl.ds(pl.multiple_of(i * row_tile, row_tile), row_tile)
        proj_ref[r, 0:4 * HG_DK] = jnp.dot(x_ref[0, r, :], w_ref[0], preferred_element_type=F32)
        return carry

    lax.fori_loop(0, seq // row_tile, project, 0)

    lb = lb_ref[0]
    nw = nw_ref[0]
    row = lax.broadcasted_iota(jnp.int32, (chunk, HG_DK), 0)
    causal = (lax.broadcasted_iota(jnp.int32, (chunk, chunk), 1)
              <= lax.broadcasted_iota(jnp.int32, (chunk, chunk), 0))

    def step(c, state_t):
        r = pl.ds(pl.multiple_of(c * chunk, chunk), chunk)
        q = proj_ref[r, 0:128]
        f = proj_ref[r, 128:256]
        v = proj_ref[r, 256:384]
        g = proj_ref[r, 384:512]
        q = q * _sigmoid(q)
        fg = lb + (1.0 - lb) * _sigmoid(f)
        k = 1.0 - fg
        b = jnp.log(fg)
        s = 1
        while s < chunk:
            b = b + jnp.where(row >= s, pltpu.roll(b, s, axis=0), 0.0)
            s *= 2
        b_last = b[chunk - 1:chunk, :]
        qd = (q * jnp.exp(b)).astype(BF16)
        kd = (k * jnp.exp(-b)).astype(BF16)
        kl = (k * jnp.exp(b_last - b)).astype(BF16)
        vb = v.astype(BF16)
        att = jnp.where(causal, _nt_dot(qd, kd), 0.0)
        o = (jnp.dot(att.astype(BF16), vb, preferred_element_type=F32)
             + _nt_dot(qd, state_t.astype(BF16)))
        state_t = state_t * jnp.exp(b_last) + _tn_dot(vb, kl)
        o = o * lax.rsqrt(jnp.mean(o * o, axis=-1, keepdims=True) + RMS_EPS)
        o_ref[0, r, :] = o * nw * (g * _sigmoid(g))
        return state_t

    lax.fori_loop(0, seq // chunk, step, jnp.zeros((HG_DV, HG_DK), F32), unroll=16)


def _hgrn2_mix(xb, w_heads, lb, norm_w):
    B, S, D = xb.shape
    H = w_heads.shape[0]
    kern = functools.partial(_hgrn2_kernel, seq=S, chunk=HG_CHUNK, row_tile=min(512, S))
    return pl.pallas_call(
        kern,
        grid=(B, H),
        in_specs=[
            pl.BlockSpec((1, S, D), lambda b, h: (b, 0, 0)),
            pl.BlockSpec((1, D, 4 * HG_DK), lambda b, h: (h, 0, 0)),
            pl.BlockSpec((1, 1, HG_DK), lambda b, h: (h, 0, 0)),
            pl.BlockSpec((1, 1, HG_DV), lambda b, h: (h, 0, 0)),
        ],
        out_specs=pl.BlockSpec((1, S, HG_DV), lambda b, h: (b, 0, h)),
        out_shape=jax.ShapeDtypeStruct((B, S, H * HG_DV), F32),
        scratch_shapes=[pltpu.VMEM((S, 4 * HG_DK + LANES), F32)],
        compiler_params=pltpu.CompilerParams(
            dimension_semantics=("parallel", "arbitrary"), vmem_limit_bytes=VMEM_LIMIT),
        name="hgrn2_mix",
    )(xb, w_heads, lb, norm_w)


def _layernorm_rows(z, g, b):
    mu = jnp.mean(z, axis=-1, keepdims=True)
    zc = z - mu
    var = jnp.mean(zc * zc, axis=-1, keepdims=True)
    return zc * lax.rsqrt(var + LN_EPS) * g + b


def _proj_res_ln_kernel(a_ref, w_ref, res_ref, g_ref, b_ref, o_ref):
    y = jnp.dot(a_ref[...].astype(BF16), w_ref[...], preferred_element_type=F32)
    o_ref[...] = _layernorm_rows(DN_ALPHA * res_ref[...] + y, g_ref[...], b_ref[...]).T


def _proj_res_ln(a, w_bf16, res, g, b, *, tm=512):
    T, K = a.shape
    N = w_bf16.shape[1]
    tm = min(tm, T)
    return pl.pallas_call(
        _proj_res_ln_kernel,
        grid=(T // tm,),
        in_specs=[
            pl.BlockSpec((tm, K), lambda i: (i, 0)),
            pl.BlockSpec((K, N), lambda i: (0, 0)),
            pl.BlockSpec((tm, N), lambda i: (i, 0)),
            pl.BlockSpec((1, N), lambda i: (0, 0)),
            pl.BlockSpec((1, N), lambda i: (0, 0)),
        ],
        out_specs=pl.BlockSpec((N, tm), lambda i: (0, i)),
        out_shape=jax.ShapeDtypeStruct((N, T), F32),
        compiler_params=pltpu.CompilerParams(
            dimension_semantics=("parallel",), vmem_limit_bytes=VMEM_LIMIT),
        name="proj_res_ln",
    )(a, w_bf16, res, g.reshape(1, N), b.reshape(1, N))


def _top16_desc(s):
    vals = []
    rank = jnp.full(s.shape, 255.0, F32)
    for q in range(PEER_TOPK):
        m = jnp.max(s, axis=0, keepdims=True)
        hit = s == m
        rank = jnp.where(hit, float(q), rank)
        s = jnp.where(hit, NEG_INF, s)
        vals.append(m)
    return vals, rank


def _rows_to_slab(vals, lo):
    n = vals[0].shape[1]
    rid = lax.broadcasted_iota(jnp.int32, (8, n), 0)
    slab = jnp.zeros((8, n), F32)
    for r in range(8):
        slab = jnp.where(rid == r, vals[lo + r], slab)
    return slab


def _peer_route_kernel(ht_ref, wq_ref, sk_ref, ga_ref, cnt_ref, gb_ref, rk_ref, q_scr, s_scr, *, tm):
    hb = ht_ref[...].astype(BF16)
    q_scr[...] = jnp.dot(wq_ref[...], hb, preferred_element_type=F32).astype(BF16)
    for hc in range(2 * PEER_HEADS):
        s_scr[hc, :, 0:tm] = jnp.dot(sk_ref[hc], q_scr[hc * PEER_HALF:(hc + 1) * PEER_HALF, :],
                                     preferred_element_type=F32)

    ncol = tm // LANES
    rid8 = lax.broadcasted_iota(jnp.int32, (8, LANES), 0)

    def body(it, carry):
        hh = it // ncol
        cc = it % ncol
        col = pl.ds(pl.multiple_of(cc * LANES, LANES), LANES)
        s1 = s_scr[2 * hh, :, col]
        s2 = s_scr[2 * hh + 1, :, col]
        a_vals, _ = _top16_desc(s1)
        b_vals, rank2 = _top16_desc(s2)
        a_lo = _rows_to_slab(a_vals, 0)
        a_hi = _rows_to_slab(a_vals, 8)
        slabs = [a_lo + b_vals[0], a_hi + b_vals[0]]
        for q in range(1, PEER_TOPK):
            keep = PEER_TOPK // (q + 1)
            slabs.append(jnp.where(rid8 < keep, a_lo + b_vals[q], NEG_INF))
        cand = jnp.concatenate(slabs, axis=0)
        for _ in range(PEER_TOPK - 1):
            m = jnp.max(cand, axis=0, keepdims=True)
            cand = jnp.where(cand == m, NEG_INF, cand)
        thr = jnp.max(cand, axis=0, keepdims=True)

        ea_lo = jnp.exp(a_lo - a_vals[0])
        ea_hi = jnp.exp(a_hi - a_vals[0])
        cnt = jnp.zeros(s1.shape, F32)
        zacc = jnp.zeros((8, LANES), F32)
        for q in range(PEER_TOPK):
            bq = b_vals[q]
            cnt = cnt + jnp.where(s1 + bq >= thr, 1.0, 0.0)
            part = (jnp.where(a_lo + bq >= thr, ea_lo, 0.0)
                    + jnp.where(a_hi + bq >= thr, ea_hi, 0.0))
            zacc = zacc + jnp.exp(bq - b_vals[0]) * part
        inv_z = 1.0 / jnp.sum(zacc, axis=0, keepdims=True)

        ga_ref[hh, cc] = jnp.exp(s1 - a_vals[0])
        cnt_ref[hh, cc] = cnt
        gb_ref[hh, cc] = (jnp.exp(s2 - b_vals[0]) * inv_z).astype(BF16)
        rk_ref[hh, cc] = rank2.astype(BF16)
        return carry

    lax.fori_loop(0, PEER_HEADS * ncol, body, 0)


def _peer_route(ht, wq_t, sub_keys, *, tm=512):
    D, T = ht.shape
    tm = min(tm, T)
    nq = wq_t.shape[0]
    kern = functools.partial(_peer_route_kernel, tm=tm)
    head_rows = pl.BlockSpec((PEER_HEADS, tm // LANES, PEER_NKEYS, LANES), lambda i: (0, i, 0, 0))
    shp = (PEER_HEADS, T // LANES, PEER_NKEYS, LANES)
    return pl.pallas_call(
        kern,
        grid=(T // tm,),
        in_specs=[
            pl.BlockSpec((D, tm), lambda i: (0, i)),
            pl.BlockSpec((nq, D), lambda i: (0, 0)),
            pl.BlockSpec((2 * PEER_HEADS, PEER_NKEYS, PEER_HALF), lambda i: (0, 0, 0)),
        ],
        out_specs=[head_rows, head_rows, head_rows, head_rows],
        out_shape=[jax.ShapeDtypeStruct(shp, F32), jax.ShapeDtypeStruct(shp, F32),
                   jax.ShapeDtypeStruct(shp, BF16), jax.ShapeDtypeStruct(shp, BF16)],
        scratch_shapes=[pltpu.VMEM((nq, tm), BF16),
                        pltpu.VMEM((2 * PEER_HEADS, PEER_NKEYS, tm + LANES), F32)],
        compiler_params=pltpu.CompilerParams(
            dimension_semantics=("parallel",), vmem_limit_bytes=VMEM_LIMIT),
        name="peer_route",
    )(ht, wq_t, sub_keys)


def _peer_expert_kernel(res_ref, u_ref, vt_ref, ga_ref, cnt_ref, gb_ref, rk_ref, g_ref,
                        b_ref, o_ref, xb_ref, a_scr, w_scr, rk16_scr, gb16_scr, acc_scr,
                        *, tm, ni):
    e = pl.program_id(1)

    @pl.when(e == 0)
    def _():
        acc_scr[:, 0:tm] = jnp.zeros((acc_scr.shape[0], tm), F32)
        xb_ref[...] = res_ref[...].astype(BF16)

        def repack_head(h, carry):
            for c in range(tm // LANES):
                rk16_scr[h, :, c * LANES:(c + 1) * LANES] = rk_ref[h, c]
                gb16_scr[h, :, c * LANES:(c + 1) * LANES] = gb_ref[h, c]
            return carry

        lax.fori_loop(0, PEER_HEADS, repack_head, 0)

    group = 4
    slab = 16
    nslab = PEER_NKEYS // slab
    wide = 2 * LANES

    def routing_weight(a_rd, w_wr, cc):
        col = pl.ds(pl.multiple_of(cc * LANES, LANES), LANES)
        for g0 in range(0, ni, group):
            accs = [[jnp.zeros((slab, LANES), BF16) for _ in range(nslab)] for _ in range(group)]
            for h in range(PEER_HEADS):
                rk = rk16_scr[h, :, col]
                gb = gb16_scr[h, :, col]
                rks = [rk[i * slab:(i + 1) * slab] for i in range(nslab)]
                gbs = [gb[i * slab:(i + 1) * slab] for i in range(nslab)]
                for k in range(group):
                    il = g0 + k
                    n16 = jnp.broadcast_to(cnt_ref[h, cc, il:il + 1, :], (slab, LANES)).astype(BF16)
                    a16 = jnp.broadcast_to(ga_ref[h, cc, il:il + 1, :], (slab, LANES)).astype(BF16)
                    for i in range(nslab):
                        accs[k][i] = accs[k][i] + jnp.where(rks[i] < n16, gbs[i] * a16,
                                                            jnp.zeros_like(a16))
            for k in range(group):
                r0 = (g0 + k) * PEER_NKEYS
                a = a_rd[r0:r0 + PEER_NKEYS, col].astype(BF16)
                gelu = (0.5 * a) * (1.0 + lax.erf(a * INV_SQRT2))
                w_wr[r0:r0 + PEER_NKEYS, col] = jnp.concatenate(accs[k], axis=0) * gelu

    a_scr[:, 0:tm] = jnp.dot(u_ref[...], xb_ref[...], preferred_element_type=F32)

    def vpu_pass(cc, carry):
        routing_weight(a_scr, w_scr, cc)
        return carry

    lax.fori_loop(0, tm // LANES, vpu_pass, 0)
    acc_scr[:, 0:tm] += jnp.dot(vt_ref[...], w_scr[:, 0:tm], preferred_element_type=F32)

    @pl.when(e == pl.num_programs(1) - 1)
    def _():
        d_model = acc_scr.shape[0]
        n_blk = d_model // 8
        inv_d = 1.0 / d_model

        def rows(i):
            return pl.ds(pl.multiple_of(i * 8, 8), 8)

        def add_z(i, acc):
            z = DN_ALPHA * res_ref[rows(i), :] + acc_scr[rows(i), 0:tm]
            acc_scr[rows(i), 0:tm] = z
            return acc + z

        zero = jnp.zeros((8, tm), F32)
        mu = jnp.sum(lax.fori_loop(0, n_blk, add_z, zero), axis=0, keepdims=True) * inv_d

        def add_sq(i, acc):
            zc = acc_scr[rows(i), 0:tm] - mu
            return acc + zc * zc

        var = jnp.sum(lax.fori_loop(0, n_blk, add_sq, zero), axis=0, keepdims=True) * inv_d
        rstd = lax.rsqrt(var + LN_EPS)
        for c in range(tm // LANES):
            col = slice(c * LANES, (c + 1) * LANES)
            y = (acc_scr[:, col] - mu[:, col]) * rstd[:, col] * g_ref[...] + b_ref[...]
            o_ref[col, :] = y.T


def _peer_experts(ht, u_all, vt_all, layer, ga, cnt, gb, rk, g, b, *, tm=1024, eb=1024):
    D, T = ht.shape
    E = u_all.shape[1]
    tm = min(tm, T)
    ni = eb // PEER_NKEYS
    kern = functools.partial(_peer_expert_kernel, tm=tm, ni=ni)
    once = pl.Buffered(1)
    return pl.pallas_call(
        kern,
        grid=(T // tm, E // eb),
        in_specs=[
            pl.BlockSpec((D, tm), lambda t, e: (0, t), pipeline_mode=once),
            pl.BlockSpec((None, eb, D), lambda t, e: (layer, e, 0)),
            pl.BlockSpec((None, D, eb), lambda t, e: (layer, 0, e)),
            pl.BlockSpec((PEER_HEADS, tm // LANES, ni, LANES), lambda t, e: (0, t, e, 0)),
            pl.BlockSpec((PEER_HEADS, tm // LANES, ni, LANES), lambda t, e: (0, t, e, 0)),
            pl.BlockSpec((PEER_HEADS, tm // LANES, PEER_NKEYS, LANES), lambda t, e: (0, t, 0, 0),
                         pipeline_mode=once),
            pl.BlockSpec((PEER_HEADS, tm // LANES, PEER_NKEYS, LANES), lambda t, e: (0, t, 0, 0),
                         pipeline_mode=once),
            pl.BlockSpec((D, 1), lambda t, e: (0, 0)),
            pl.BlockSpec((D, 1), lambda t, e: (0, 0)),
        ],
        out_specs=pl.BlockSpec((tm, D), lambda t, e: (t, 0)),
        out_shape=jax.ShapeDtypeStruct((T, D), F32),
        scratch_shapes=[pltpu.VMEM((D, tm), BF16),
                        pltpu.VMEM((eb, tm + LANES), F32), pltpu.VMEM((eb, tm + LANES), BF16),
                        pltpu.VMEM((PEER_HEADS, PEER_NKEYS, tm + LANES), BF16),
                        pltpu.VMEM((PEER_HEADS, PEER_NKEYS, tm + LANES), BF16),
                        pltpu.VMEM((D, tm + LANES), F32)],
        compiler_params=pltpu.CompilerParams(
            dimension_semantics=("parallel", "arbitrary"), vmem_limit_bytes=VMEM_LIMIT),
        name="peer_experts",
    )(ht, u_all, vt_all, ga, cnt, gb, rk, g.reshape(D, 1), b.reshape(D, 1))


def _peer_layer(ht, layer, w_pq, sub_keys, u_all, vt_all, g, b):
    wq_t = w_pq.T.astype(BF16)
    sk = sub_keys.reshape(2 * PEER_HEADS, PEER_NKEYS, PEER_HALF).astype(BF16)
    ga, cnt, gb, rk = _peer_route(ht, wq_t, sk)
    return _peer_experts(ht, u_all, vt_all, layer, ga, cnt, gb, rk, g, b)


def _rope_lane_tables():
    d = jnp.arange(LANES) % SW_HEAD_DIM
    half = ROPE_DIMS // 2
    inv = ROPE_THETA ** (-(2.0 * (d % half).astype(F32)) / ROPE_DIMS)
    inv = jnp.where(d < ROPE_DIMS, inv, 0.0)
    sgn_up = jnp.where(d < half, -1.0, 0.0)
    sgn_dn = jnp.where((d >= half) & (d < ROPE_DIMS), 1.0, 0.0)
    return jnp.stack([inv, sgn_up, sgn_dn]).astype(F32)


def _rope_apply(x, pos_f, tab):
    n, width = x.shape
    reps = width // LANES
    ang = pos_f * tab[0:1, :]
    cos = jnp.tile(jnp.cos(ang), (1, reps))
    sin = jnp.tile(jnp.sin(ang), (1, reps))
    s_up = jnp.tile(tab[1:2, :], (1, reps))
    s_dn = jnp.tile(tab[2:3, :], (1, reps))
    half = ROPE_DIMS // 2
    partner = pltpu.roll(x, width - half, axis=1) * s_up + pltpu.roll(x, half, axis=1) * s_dn
    return x * cos + partner * sin


def _kv_kernel(h_ref, w_ref, pos_ref, tab_ref, k_ref, v_ref):
    kv = jnp.dot(h_ref[...].astype(BF16), w_ref[...], preferred_element_type=F32)
    nk2 = SW_KV_HEADS * LANES
    k_ref[...] = _rope_apply(kv[:, :nk2], pos_ref[...].astype(F32), tab_ref[...]).astype(BF16)
    v_ref[...] = kv[:, nk2:].astype(BF16)


def _dup_heads(w):
    D, n = w.shape
    w3 = w.reshape(D, n // SW_HEAD_DIM, SW_HEAD_DIM)
    return jnp.concatenate([w3, w3], axis=2).reshape(D, 2 * n)


def _shared_kv(h, w_kv, pos, tab, *, tm=512):
    T, D = h.shape
    tm = min(tm, T)
    nk = SW_KV_HEADS * SW_HEAD_DIM
    nk2 = SW_KV_HEADS * LANES
    w_kv_bf16 = jnp.concatenate([_dup_heads(w_kv[:, :nk]), _dup_heads(w_kv[:, nk:])],
                                axis=1).astype(BF16)
    return pl.pallas_call(
        _kv_kernel,
        grid=(T // tm,),
        in_specs=[
            pl.BlockSpec((tm, D), lambda i: (i, 0)),
            pl.BlockSpec(w_kv_bf16.shape, lambda i: (0, 0)),
            pl.BlockSpec((tm, 1), lambda i: (i, 0)),
            pl.BlockSpec((3, LANES), lambda i: (0, 0)),
        ],
        out_specs=[pl.BlockSpec((tm, nk2), lambda i: (i, 0)),
                   pl.BlockSpec((tm, nk2), lambda i: (i, 0))],
        out_shape=[jax.ShapeDtypeStruct((T, nk2), BF16), jax.ShapeDtypeStruct((T, nk2), BF16)],
        compiler_params=pltpu.CompilerParams(
            dimension_semantics=("parallel",), vmem_limit_bytes=VMEM_LIMIT),
        name="shared_kv",
    )(h, w_kv_bf16, pos, tab)


def _swa_kernel(sink_ref, h_ref, wq_ref, k_ref, v_ref, pos_ref, tab_ref, wo_ref, g_ref, b_ref,
                o_ref, q_scr, att_scr, *, tq, seq):
    j = pl.program_id(1)
    W = SW_WINDOW
    hq = h_ref[...]
    q = jnp.dot(hq.astype(BF16), wq_ref[...], preferred_element_type=F32)
    q = _rope_apply(q, pos_ref[...].astype(F32), tab_ref[...]) * (SW_HEAD_DIM ** -0.5)
    q_scr[...] = q.astype(BF16)

    qi = lax.broadcasted_iota(jnp.int32, (W, 2 * W), 0)
    kj = lax.broadcasted_iota(jnp.int32, (W, 2 * W), 1)
    lane = lax.broadcasted_iota(jnp.int32, (W, LANES), 1)
    first_half = lane < SW_HEAD_DIM
    group = SW_Q_HEADS // SW_KV_HEADS

    def block(n, carry):
        q0 = j * tq + n * W
        k0 = jnp.maximum(q0 - W, 0)
        rows = pl.ds(pl.multiple_of(n * W, W), W)
        band = pl.ds(pl.multiple_of(k0, W), 2 * W)
        diff = (q0 + qi) - (k0 + kj)
        valid = (diff >= 0) & (diff < W)
        for pair in range(SW_Q_HEADS // 2):
            kvh = (2 * pair) // group
            lanes = slice(pair * LANES, (pair + 1) * LANES)
            qp = q_scr[rows, lanes]
            kb = k_ref[0, band, kvh * LANES:(kvh + 1) * LANES]
            vb = v_ref[0, band, kvh * LANES:(kvh + 1) * LANES]
            outs = []
            for sub in range(2):
                head = 2 * pair + sub
                keep = first_half if sub == 0 else jnp.logical_not(first_half)
                qh = jnp.where(keep, qp, jnp.zeros_like(qp))
                logits = jnp.where(valid, _nt_dot(qh, kb), NEG_INF)
                sink = sink_ref[head]
                m = jnp.maximum(jnp.max(logits, axis=-1, keepdims=True), sink)
                p = jnp.exp(logits - m)
                denom = jnp.sum(p, axis=-1, keepdims=True) + jnp.exp(sink - m)
                p = p * (1.0 / denom)
                outs.append(jnp.dot(p.astype(BF16), vb, preferred_element_type=F32))
            att_scr[rows, lanes] = jnp.where(first_half, outs[0], outs[1]).astype(BF16)
        return carry

    lax.fori_loop(0, tq // W, block, 0)
    y = jnp.dot(att_scr[...], wo_ref[...], preferred_element_type=F32)
    o_ref[...] = _layernorm_rows(DN_ALPHA * hq + y, g_ref[...], b_ref[...]).T


def _swa_layer(h, k_dup, v_dup, pos, tab, w_q_bf16, sinks, w_o_bf16, g, b, *, batch, tq=512):
    T, D = h.shape
    S = T // batch
    tq = min(tq, S)
    nq = S // tq
    nk2 = k_dup.shape[1]
    k3 = k_dup.reshape(batch, S, nk2)
    v3 = v_dup.reshape(batch, S, nk2)
    kern = functools.partial(_swa_kernel, tq=tq, seq=S)
    return pl.pallas_call(
        kern,
        grid=(batch, nq),
        in_specs=[
            pl.BlockSpec(memory_space=pltpu.SMEM),
            pl.BlockSpec((tq, D), lambda bb, j: (bb * nq + j, 0)),
            pl.BlockSpec((D, D), lambda bb, j: (0, 0)),
            pl.BlockSpec((1, S, nk2), lambda bb, j: (bb, 0, 0)),
            pl.BlockSpec((1, S, nk2), lambda bb, j: (bb, 0, 0)),
            pl.BlockSpec((tq, 1), lambda bb, j: (bb * nq + j, 0)),
            pl.BlockSpec((3, LANES), lambda bb, j: (0, 0)),
            pl.BlockSpec((D, D), lambda bb, j: (0, 0)),
            pl.BlockSpec((1, D), lambda bb, j: (0, 0)),
            pl.BlockSpec((1, D), lambda bb, j: (0, 0)),
        ],
        out_specs=pl.BlockSpec((D, tq), lambda bb, j: (0, bb * nq + j)),
        out_shape=jax.ShapeDtypeStruct((D, T), F32),
        scratch_shapes=[pltpu.VMEM((tq, D), BF16), pltpu.VMEM((tq, D), BF16)],
        compiler_params=pltpu.CompilerParams(
            dimension_semantics=("parallel", "arbitrary"), vmem_limit_bytes=VMEM_LIMIT),
        name="swa_layer",
    )(sinks, h, w_q_bf16, k3, v3, pos, tab, w_o_bf16, g.reshape(1, D), b.reshape(1, D))


def kernel(x, positions, hg_w_in, hg_lb, hg_norm_w, hg_w_o, sw_w_q, sw_sinks, sw_w_o, w_kv,
           peer_w_q, peer_sub_keys, peer_u, peer_v, ln_mix_g, ln_mix_b, ln_ffn_g, ln_ffn_b):
    B, S, D = x.shape
    T = B * S
    H = HG_HEADS
    pos = positions.reshape(T, 1)
    tab = _rope_lane_tables()

    lb_all = jnp.cumsum(jax.nn.softmax(hg_lb.astype(F32), axis=0), axis=0)
    w_heads = (hg_w_in[0].reshape(D, 4, H, HG_DK).transpose(2, 0, 1, 3)
               .reshape(H, D, 4 * HG_DK).astype(BF16))
    mix = _hgrn2_mix(x.astype(BF16), w_heads, lb_all[0].reshape(H, 1, HG_DK),
                     hg_norm_w[0].reshape(H, 1, HG_DV))
    ht = _proj_res_ln(mix.reshape(T, H * HG_DV), hg_w_o[0].astype(BF16), x.reshape(T, D),
                      ln_mix_g[0], ln_mix_b[0])
    u_all = peer_u.astype(BF16)
    vt_all = jnp.swapaxes(peer_v, 1, 2).astype(BF16)
    h = _peer_layer(ht, 0, peer_w_q[0], peer_sub_keys[0], u_all, vt_all,
                    ln_ffn_g[0], ln_ffn_b[0])

    k_dup, v_dup = _shared_kv(h, w_kv, pos, tab)
    ht = _swa_layer(h, k_dup, v_dup, pos, tab, sw_w_q[0].astype(BF16), sw_sinks[0].astype(F32),
                    sw_w_o[0].astype(BF16), ln_mix_g[1], ln_mix_b[1], batch=B)
    h = _peer_layer(ht, 1, peer_w_q[1], peer_sub_keys[1], u_all, vt_all,
                    ln_ffn_g[1], ln_ffn_b[1])
    return h.reshape(B, S, D)
```

```python
import functools
import math

import jax
import jax.numpy as jnp
from jax import lax
from jax.experimental import pallas as pl
from jax.experimental.pallas import tpu as pltpu

F32 = jnp.float32
BF16 = jnp.bfloat16

D_MODEL = 1024
DEPTH = 2
HG_HEADS = 8
HG_DK = 128
HG_DV = 128
HG_CHUNK = 32
SW_Q_HEADS = 16
SW_KV_HEADS = 4
SW_HEAD_DIM = 64
SW_WINDOW = 128
ROPE_THETA = 500000.0
ROPE_DIMS = 16
PEER_HEADS = 8
PEER_NKEYS = 128
PEER_HALF = 128
PEER_TOPK = 16
DN_ALPHA = (2.0 * DEPTH) ** 0.25
LN_EPS = 1e-5
RMS_EPS = 1e-6

LANES = 128
VMEM_LIMIT = 56 * 1024 * 1024
NEG_INF = float("-inf")
INV_SQRT2 = 1.0 / math.sqrt(2.0)


def _sigmoid(x):
    return 1.0 / (1.0 + jnp.exp(-x))


def _nt_dot(a, b):
    return lax.dot_general(a, b, (((1,), (1,)), ((), ())), preferred_element_type=F32)


def _tn_dot(a, b):
    return lax.dot_general(a, b, (((0,), (0,)), ((), ())), preferred_element_type=F32)


def _hgrn2_kernel(x_ref, w_ref, lb_ref, nw_ref, o_ref, proj_ref, *, seq, chunk, row_tile):
    def project(i, carry):
        r = pl.ds(pl.multiple_of(i * row_tile, row_tile), row_tile)
        proj_ref[r, 0:4 * HG_DK] = jnp.dot(x_ref[0, r, :], w_ref[0], preferred_element_type=F32)
        return carry

    lax.fori_loop(0, seq // row_tile, project, 0)

    lb = lb_ref[0]
    nw = nw_ref[0]
    row = lax.broadcasted_iota(jnp.int32, (chunk, HG_DK), 0)
    causal = (lax.broadcasted_iota(jnp.int32, (chunk, chunk), 1)
              <= lax.broadcasted_iota(jnp.int32, (chunk, chunk), 0))

    def step(c, state_t):
        r = pl.ds(pl.multiple_of(c * chunk, chunk), chunk)
        q = proj_ref[r, 0:128]
        f = proj_ref[r, 128:256]
        v = proj_ref[r, 256:384]
        g = proj_ref[r, 384:512]
        q = q * _sigmoid(q)
        fg = lb + (1.0 - lb) * _sigmoid(f)
        k = 1.0 - fg
        b = jnp.log(fg)
        s = 1
        while s < chunk:
            b = b + jnp.where(row >= s, pltpu.roll(b, s, axis=0), 0.0)
            s *= 2
        b_last = b[chunk - 1:chunk, :]
        qd = (q * jnp.exp(b)).astype(BF16)
        kd = (k * jnp.exp(-b)).astype(BF16)
        kl = (k * jnp.exp(b_last - b)).astype(BF16)
        vb = v.astype(BF16)
        att = jnp.where(causal, _nt_dot(qd, kd), 0.0)
        o = (jnp.dot(att.astype(BF16), vb, preferred_element_type=F32)
             + _nt_dot(qd, state_t.astype(BF16)))
        state_t = state_t * jnp.exp(b_last) + _tn_dot(vb, kl)
        o = o * lax.rsqrt(jnp.mean(o * o, axis=-1, keepdims=True) + RMS_EPS)
        o_ref[0, r, :] = o * nw * (g * _sigmoid(g))
        return state_t

    lax.fori_loop(0, seq // chunk, step, jnp.zeros((HG_DV, HG_DK), F32), unroll=16)


def _hgrn2_mix(xb, w_heads, lb, norm_w):
    B, S, D = xb.shape
    H = w_heads.shape[0]
    kern = functools.partial(_hgrn2_kernel, seq=S, chunk=HG_CHUNK, row_tile=min(512, S))
    return pl.pallas_call(
        kern,
        grid=(B, H),
        in_specs=[
            pl.BlockSpec((1, S, D), lambda b, h: (b, 0, 0)),
            pl.BlockSpec((1, D, 4 * HG_DK), lambda b, h: (h, 0, 0)),
            pl.BlockSpec((1, 1, HG_DK), lambda b, h: (h, 0, 0)),
            pl.BlockSpec((1, 1, HG_DV), lambda b, h: (h, 0, 0)),
        ],
        out_specs=pl.BlockSpec((1, S, HG_DV), lambda b, h: (b, 0, h)),
        out_shape=jax.ShapeDtypeStruct((B, S, H * HG_DV), F32),
        scratch_shapes=[pltpu.VMEM((S, 4 * HG_DK + LANES), F32)],
        compiler_params=pltpu.CompilerParams(
            dimension_semantics=("parallel", "arbitrary"), vmem_limit_bytes=VMEM_LIMIT),
        name="hgrn2_mix",
    )(xb, w_heads, lb, norm_w)


def _layernorm_rows(z, g, b):
    mu = jnp.mean(z, axis=-1, keepdims=True)
    zc = z - mu
    var = jnp.mean(zc * zc, axis=-1, keepdims=True)
    return zc * lax.rsqrt(var + LN_EPS) * g + b


def _proj_res_ln_kernel(a_ref, w_ref, res_ref, g_ref, b_ref, o_ref):
    y = jnp.dot(a_ref[...].astype(BF16), w_ref[...], preferred_element_type=F32)
    o_ref[...] = _layernorm_rows(DN_ALPHA * res_ref[...] + y, g_ref[...], b_ref[...]).T


def _proj_res_ln(a, w_bf16, res, g, b, *, tm=512):
    T, K = a.shape
    N = w_bf16.shape[1]
    tm = min(tm, T)
    return pl.pallas_call(
        _proj_res_ln_kernel,
        grid=(T // tm,),
        in_specs=[
            pl.BlockSpec((tm, K), lambda i: (i, 0)),
            pl.BlockSpec((K, N), lambda i: (0, 0)),
            pl.BlockSpec((tm, N), lambda i: (i, 0)),
            pl.BlockSpec((1, N), lambda i: (0, 0)),
            pl.BlockSpec((1, N), lambda i: (0, 0)),
        ],
        out_specs=pl.BlockSpec((N, tm), lambda i: (0, i)),
        out_shape=jax.ShapeDtypeStruct((N, T), F32),
        compiler_params=pltpu.CompilerParams(
            dimension_semantics=("parallel",), vmem_limit_bytes=VMEM_LIMIT),
        name="proj_res_ln",
    )(a, w_bf16, res, g.reshape(1, N), b.reshape(1, N))


def _sorting_network(n):
    pairs = []

    def merge(lo, cnt, r):
        step = 2 * r
        if step < cnt:
            merge(lo, cnt, step)
            merge(lo + r, cnt, step)
            for i in range(lo + r, lo + cnt - r, step):
                pairs.append((i, i + r))
        else:
            pairs.append((lo, lo + r))

    def sort(lo, cnt):
        if cnt > 1:
            half = cnt // 2
            sort(lo, half)
            sort(lo + half, half)
            merge(lo, cnt, 1)

    sort(0, n)
    return pairs


_SORT16 = _sorting_network(PEER_TOPK)


def _pop_sorted_lists(lists, extra, n_pop):
    lists = list(lists)
    vals = []
    for t in range(n_pop):
        head = lists[0] if extra is None else jnp.maximum(lists[0], extra)
        m = jnp.max(head, axis=0, keepdims=True)
        vals.append(m)
        remaining = n_pop - 1 - t
        if remaining == 0:
            break
        hit = lists[0] == m
        for k in range(min(remaining, len(lists) - 1)):
            lists[k] = jnp.where(hit, lists[k + 1], lists[k])
        if remaining >= len(lists):
            lists[len(lists) - 1] = jnp.where(hit, NEG_INF, lists[len(lists) - 1])
        if extra is not None:
            extra = jnp.where(extra == m, NEG_INF, extra)
    return vals


def _top16_desc(s):
    v = [s[8 * k:8 * (k + 1)] for k in range(PEER_NKEYS // 8)]
    for i, j in _SORT16:
        v[i], v[j] = jnp.maximum(v[i], v[j]), jnp.minimum(v[i], v[j])
    return _pop_sorted_lists(v, None, PEER_TOPK)


def _rows_to_slab(vals, lo):
    n = vals[0].shape[1]
    rid = lax.broadcasted_iota(jnp.int32, (8, n), 0)
    slab = jnp.zeros((8, n), F32)
    for r in range(8):
        slab = jnp.where(rid == r, vals[lo + r], slab)
    return slab


def _peer_route_kernel(ht_ref, wq_ref, sk_ref, ga_ref, cnt_ref, gb_ref, rk_ref, q_scr, s_scr, *, tm):
    hb = ht_ref[...].astype(BF16)
    q_scr[...] = jnp.dot(wq_ref[...], hb, preferred_element_type=F32).astype(BF16)
    for hc in range(2 * PEER_HEADS):
        s_scr[hc, :, 0:tm] = jnp.dot(sk_ref[hc], q_scr[hc * PEER_HALF:(hc + 1) * PEER_HALF, :],
                                     preferred_element_type=F32)

    ncol = tm // LANES
    rid8 = lax.broadcasted_iota(jnp.int32, (8, LANES), 0)

    def body(it, carry):
        hh = it // ncol
        cc = it % ncol
        col = pl.ds(pl.multiple_of(cc * LANES, LANES), LANES)
        s1 = s_scr[2 * hh, :, col]
        s2 = s_scr[2 * hh + 1, :, col]
        a_vals = _top16_desc(s1)
        b_vals = _top16_desc(s2)
        rank2 = jnp.zeros(s2.shape, F32)
        for q in range(PEER_TOPK):
            rank2 = rank2 + jnp.where(s2 < b_vals[q], 1.0, 0.0)
        a_lo = _rows_to_slab(a_vals, 0)
        a_hi = _rows_to_slab(a_vals, 8)
        b_lo = _rows_to_slab(b_vals, 0)
        b_hi = _rows_to_slab(b_vals, 8)
        lists = []
        for p in range(PEER_TOPK):
            n_lists = sum(1 for q in range(8) if PEER_TOPK // (q + 1) > p)
            lists.append(jnp.where(rid8 < n_lists, a_vals[p] + b_lo, NEG_INF))
        thr = _pop_sorted_lists(lists, a_vals[0] + b_hi, PEER_TOPK)[-1]

        ea_lo = jnp.exp(a_lo - a_vals[0])
        ea_hi = jnp.exp(a_hi - a_vals[0])
        cnt = jnp.zeros(s1.shape, F32)
        zacc = jnp.zeros((8, LANES), F32)
        for q in range(PEER_TOPK):
            bq = b_vals[q]
            cnt = cnt + jnp.where(s1 + bq >= thr, 1.0, 0.0)
            part = (jnp.where(a_lo + bq >= thr, ea_lo, 0.0)
                    + jnp.where(a_hi + bq >= thr, ea_hi, 0.0))
            zacc = zacc + jnp.exp(bq - b_vals[0]) * part
        inv_z = 1.0 / jnp.sum(zacc, axis=0, keepdims=True)

        ga_ref[hh, cc] = jnp.exp(s1 - a_vals[0])
        cnt_ref[hh, cc] = cnt
        gb_ref[hh, cc] = (jnp.exp(s2 - b_vals[0]) * inv_z).astype(BF16)
        rk_ref[hh, cc] = rank2.astype(BF16)
        return carry

    lax.fori_loop(0, PEER_HEADS * ncol, body, 0)


def _peer_route(ht, wq_t, sub_keys, *, tm=512):
    D, T = ht.shape
    tm = min(tm, T)
    nq = wq_t.shape[0]
    kern = functools.partial(_peer_route_kernel, tm=tm)
    head_rows = pl.BlockSpec((PEER_HEADS, tm // LANES, PEER_NKEYS, LANES), lambda i: (0, i, 0, 0))
    shp = (PEER_HEADS, T // LANES, PEER_NKEYS, LANES)
    return pl.pallas_call(
        kern,
        grid=(T // tm,),
        in_specs=[
            pl.BlockSpec((D, tm), lambda i: (0, i)),
            pl.BlockSpec((nq, D), lambda i: (0, 0)),
            pl.BlockSpec((2 * PEER_HEADS, PEER_NKEYS, PEER_HALF), lambda i: (0, 0, 0)),
        ],
        out_specs=[head_rows, head_rows, head_rows, head_rows],
        out_shape=[jax.ShapeDtypeStruct(shp, F32), jax.ShapeDtypeStruct(shp, F32),
                   jax.ShapeDtypeStruct(shp, BF16), jax.ShapeDtypeStruct(shp, BF16)],
        scratch_shapes=[pltpu.VMEM((nq, tm), BF16),
                        pltpu.VMEM((2 * PEER_HEADS, PEER_NKEYS, tm + LANES), F32)],
        compiler_params=pltpu.CompilerParams(
            dimension_semantics=("parallel",), vmem_limit_bytes=VMEM_LIMIT),
        name="peer_route",
    )(ht, wq_t, sub_keys)


def _peer_expert_kernel(res_ref, u_ref, vt_ref, ga_ref, cnt_ref, gb_ref, rk_ref, g_ref,
                        b_ref, o_ref, xb_ref, a_scr, w_scr, rk16_scr, gb16_scr, acc_scr,
                        *, tm, ni):
    e = pl.program_id(1)

    @pl.when(e == 0)
    def _():
        acc_scr[:, 0:tm] = jnp.zeros((acc_scr.shape[0], tm), F32)
        xb_ref[...] = res_ref[...].astype(BF16)

        def repack_head(h, carry):
            for c in range(tm // LANES):
                rk16_scr[h, :, c * LANES:(c + 1) * LANES] = rk_ref[h, c]
                gb16_scr[h, :, c * LANES:(c + 1) * LANES] = gb_ref[h, c]
            return carry

        lax.fori_loop(0, PEER_HEADS, repack_head, 0)

    group = 4
    slab = 16
    nslab = PEER_NKEYS // slab
    wide = 2 * LANES

    def routing_weight(a_rd, w_wr, cc):
        col = pl.ds(pl.multiple_of(cc * LANES, LANES), LANES)
        for g0 in range(0, ni, group):
            accs = [[jnp.zeros((slab, LANES), BF16) for _ in range(nslab)] for _ in range(group)]
            for h in range(PEER_HEADS):
                rk = rk16_scr[h, :, col]
                gb = gb16_scr[h, :, col]
                rks = [rk[i * slab:(i + 1) * slab] for i in range(nslab)]
                gbs = [gb[i * slab:(i + 1) * slab] for i in range(nslab)]
                for k in range(group):
                    il = g0 + k
                    n16 = jnp.broadcast_to(cnt_ref[h, cc, il:il + 1, :], (slab, LANES)).astype(BF16)
                    a16 = jnp.broadcast_to(ga_ref[h, cc, il:il + 1, :], (slab, LANES)).astype(BF16)
                    for i in range(nslab):
                        accs[k][i] = accs[k][i] + jnp.where(rks[i] < n16, gbs[i] * a16,
                                                            jnp.zeros_like(a16))
            for k in range(group):
                r0 = (g0 + k) * PEER_NKEYS
                a = a_rd[r0:r0 + PEER_NKEYS, col].astype(BF16)
                gelu = (0.5 * a) * (1.0 + lax.erf(a * INV_SQRT2))
                w_wr[r0:r0 + PEER_NKEYS, col] = jnp.concatenate(accs[k], axis=0) * gelu

    a_scr[:, 0:tm] = jnp.dot(u_ref[...], xb_ref[...], preferred_element_type=F32)

    def vpu_pass(cc, carry):
        routing_weight(a_scr, w_scr, cc)
        return carry

    lax.fori_loop(0, tm // LANES, vpu_pass, 0)
    acc_scr[:, 0:tm] += jnp.dot(vt_ref[...], w_scr[:, 0:tm], preferred_element_type=F32)

    @pl.when(e == pl.num_programs(1) - 1)
    def _():
        d_model = acc_scr.shape[0]
        n_blk = d_model // 8
        inv_d = 1.0 / d_model

        def rows(i):
            return pl.ds(pl.multiple_of(i * 8, 8), 8)

        def add_z(i, acc):
            z = DN_ALPHA * res_ref[rows(i), :] + acc_scr[rows(i), 0:tm]
            acc_scr[rows(i), 0:tm] = z
            return acc + z

        zero = jnp.zeros((8, tm), F32)
        mu = jnp.sum(lax.fori_loop(0, n_blk, add_z, zero), axis=0, keepdims=True) * inv_d

        def add_sq(i, acc):
            zc = acc_scr[rows(i), 0:tm] - mu
            return acc + zc * zc

        var = jnp.sum(lax.fori_loop(0, n_blk, add_sq, zero), axis=0, keepdims=True) * inv_d
        rstd = lax.rsqrt(var + LN_EPS)
        for c in range(tm // LANES):
            col = slice(c * LANES, (c + 1) * LANES)
            y = (acc_scr[:, col] - mu[:, col]) * rstd[:, col] * g_ref[...] + b_ref[...]
            o_ref[col, :] = y.T


def _peer_experts(ht, u_all, vt_all, layer, ga, cnt, gb, rk, g, b, *, tm=1024, eb=1024):
    D, T = ht.shape
    E = u_all.shape[1]
    tm = min(tm, T)
    ni = eb // PEER_NKEYS
    kern = functools.partial(_peer_expert_kernel, tm=tm, ni=ni)
    once = pl.Buffered(1)
    return pl.pallas_call(
        kern,
        grid=(T // tm, E // eb),
        in_specs=[
            pl.BlockSpec((D, tm), lambda t, e: (0, t), pipeline_mode=once),
            pl.BlockSpec((None, eb, D), lambda t, e: (layer, e, 0)),
            pl.BlockSpec((None, D, eb), lambda t, e: (layer, 0, e)),
            pl.BlockSpec((PEER_HEADS, tm // LANES, ni, LANES), lambda t, e: (0, t, e, 0)),
            pl.BlockSpec((PEER_HEADS, tm // LANES, ni, LANES), lambda t, e: (0, t, e, 0)),
            pl.BlockSpec((PEER_HEADS, tm // LANES, PEER_NKEYS, LANES), lambda t, e: (0, t, 0, 0),
                         pipeline_mode=once),
            pl.BlockSpec((PEER_HEADS, tm // LANES, PEER_NKEYS, LANES), lambda t, e: (0, t, 0, 0),
                         pipeline_mode=once),
            pl.BlockSpec((D, 1), lambda t, e: (0, 0)),
            pl.BlockSpec((D, 1), lambda t, e: (0, 0)),
        ],
        out_specs=pl.BlockSpec((tm, D), lambda t, e: (t, 0)),
        out_shape=jax.ShapeDtypeStruct((T, D), F32),
        scratch_shapes=[pltpu.VMEM((D, tm), BF16),
                        pltpu.VMEM((eb, tm + LANES), F32), pltpu.VMEM((eb, tm + LANES), BF16),
                        pltpu.VMEM((PEER_HEADS, PEER_NKEYS, tm + LANES), BF16),
                        pltpu.VMEM((PEER_HEADS, PEER_NKEYS, tm + LANES), BF16),
                        pltpu.VMEM((D, tm + LANES), F32)],
        compiler_params=pltpu.CompilerParams(
            dimension_semantics=("parallel", "arbitrary"), vmem_limit_bytes=VMEM_LIMIT),
        name="peer_experts",
    )(ht, u_all, vt_all, ga, cnt, gb, rk, g.reshape(D, 1), b.reshape(D, 1))


def _peer_layer(ht, layer, w_pq, sub_keys, u_all, vt_all, g, b):
    wq_t = w_pq.T.astype(BF16)
    sk = sub_keys.reshape(2 * PEER_HEADS, PEER_NKEYS, PEER_HALF).astype(BF16)
    ga, cnt, gb, rk = _peer_route(ht, wq_t, sk)
    return _peer_experts(ht, u_all, vt_all, layer, ga, cnt, gb, rk, g, b)


def _rope_lane_tables():
    d = jnp.arange(LANES) % SW_HEAD_DIM
    half = ROPE_DIMS // 2
    inv = ROPE_THETA ** (-(2.0 * (d % half).astype(F32)) / ROPE_DIMS)
    inv = jnp.where(d < ROPE_DIMS, inv, 0.0)
    sgn_up = jnp.where(d < half, -1.0, 0.0)
    sgn_dn = jnp.where((d >= half) & (d < ROPE_DIMS), 1.0, 0.0)
    return jnp.stack([inv, sgn_up, sgn_dn]).astype(F32)


def _rope_apply(x, pos_f, tab):
    n, width = x.shape
    reps = width // LANES
    ang = pos_f * tab[0:1, :]
    cos = jnp.tile(jnp.cos(ang), (1, reps))
    sin = jnp.tile(jnp.sin(ang), (1, reps))
    s_up = jnp.tile(tab[1:2, :], (1, reps))
    s_dn = jnp.tile(tab[2:3, :], (1, reps))
    half = ROPE_DIMS // 2
    partner = pltpu.roll(x, width - half, axis=1) * s_up + pltpu.roll(x, half, axis=1) * s_dn
    return x * cos + partner * sin


def _kv_kernel(h_ref, w_ref, pos_ref, tab_ref, k_ref, v_ref):
    kv = jnp.dot(h_ref[...].astype(BF16), w_ref[...], preferred_element_type=F32)
    nk2 = SW_KV_HEADS * LANES
    k_ref[...] = _rope_apply(kv[:, :nk2], pos_ref[...].astype(F32), tab_ref[...]).astype(BF16)
    v_ref[...] = kv[:, nk2:].astype(BF16)


def _dup_heads(w):
    D, n = w.shape
    w3 = w.reshape(D, n // SW_HEAD_DIM, SW_HEAD_DIM)
    return jnp.concatenate([w3, w3], axis=2).reshape(D, 2 * n)


def _shared_kv(h, w_kv, pos, tab, *, tm=512):
    T, D = h.shape
    tm = min(tm, T)
    nk = SW_KV_HEADS * SW_HEAD_DIM
    nk2 = SW_KV_HEADS * LANES
    w_kv_bf16 = jnp.concatenate([_dup_heads(w_kv[:, :nk]), _dup_heads(w_kv[:, nk:])],
                                axis=1).astype(BF16)
    return pl.pallas_call(
        _kv_kernel,
        grid=(T // tm,),
        in_specs=[
            pl.BlockSpec((tm, D), lambda i: (i, 0)),
            pl.BlockSpec(w_kv_bf16.shape, lambda i: (0, 0)),
            pl.BlockSpec((tm, 1), lambda i: (i, 0)),
            pl.BlockSpec((3, LANES), lambda i: (0, 0)),
        ],
        out_specs=[pl.BlockSpec((tm, nk2), lambda i: (i, 0)),
                   pl.BlockSpec((tm, nk2), lambda i: (i, 0))],
        out_shape=[jax.ShapeDtypeStruct((T, nk2), BF16), jax.ShapeDtypeStruct((T, nk2), BF16)],
        compiler_params=pltpu.CompilerParams(
            dimension_semantics=("parallel",), vmem_limit_bytes=VMEM_LIMIT),
        name="shared_kv",
    )(h, w_kv_bf16, pos, tab)


def _swa_kernel(sink_ref, h_ref, wq_ref, k_ref, v_ref, pos_ref, tab_ref, wo_ref, g_ref, b_ref,
                o_ref, q_scr, att_scr, *, tq, seq):
    j = pl.program_id(1)
    W = SW_WINDOW
    hq = h_ref[...]
    q = jnp.dot(hq.astype(BF16), wq_ref[...], preferred_element_type=F32)
    q = _rope_apply(q, pos_ref[...].astype(F32), tab_ref[...]) * (SW_HEAD_DIM ** -0.5)
    q_scr[...] = q.astype(BF16)

    qi = lax.broadcasted_iota(jnp.int32, (W, 2 * W), 0)
    kj = lax.broadcasted_iota(jnp.int32, (W, 2 * W), 1)
    lane = lax.broadcasted_iota(jnp.int32, (W, LANES), 1)
    first_half = lane < SW_HEAD_DIM
    group = SW_Q_HEADS // SW_KV_HEADS

    def block(n, carry):
        q0 = j * tq + n * W
        k0 = jnp.maximum(q0 - W, 0)
        rows = pl.ds(pl.multiple_of(n * W, W), W)
        band = pl.ds(pl.multiple_of(k0, W), 2 * W)
        diff = (q0 + qi) - (k0 + kj)
        valid = (diff >= 0) & (diff < W)
        for pair in range(SW_Q_HEADS // 2):
            kvh = (2 * pair) // group
            lanes = slice(pair * LANES, (pair + 1) * LANES)
            qp = q_scr[rows, lanes]
            kb = k_ref[0, band, kvh * LANES:(kvh + 1) * LANES]
            vb = v_ref[0, band, kvh * LANES:(kvh + 1) * LANES]
            outs = []
            for sub in range(2):
                head = 2 * pair + sub
                keep = first_half if sub == 0 else jnp.logical_not(first_half)
                qh = jnp.where(keep, qp, jnp.zeros_like(qp))
                logits = jnp.where(valid, _nt_dot(qh, kb), NEG_INF)
                sink = sink_ref[head]
                m = jnp.maximum(jnp.max(logits, axis=-1, keepdims=True), sink)
                p = jnp.exp(logits - m)
                denom = jnp.sum(p, axis=-1, keepdims=True) + jnp.exp(sink - m)
                p = p * (1.0 / denom)
                outs.append(jnp.dot(p.astype(BF16), vb, preferred_element_type=F32))
            att_scr[rows, lanes] = jnp.where(first_half, outs[0], outs[1]).astype(BF16)
        return carry

    lax.fori_loop(0, tq // W, block, 0)
    y = jnp.dot(att_scr[...], wo_ref[...], preferred_element_type=F32)
    o_ref[...] = _layernorm_rows(DN_ALPHA * hq + y, g_ref[...], b_ref[...]).T


def _swa_layer(h, k_dup, v_dup, pos, tab, w_q_bf16, sinks, w_o_bf16, g, b, *, batch, tq=512):
    T, D = h.shape
    S = T // batch
    tq = min(tq, S)
    nq = S // tq
    nk2 = k_dup.shape[1]
    k3 = k_dup.reshape(batch, S, nk2)
    v3 = v_dup.reshape(batch, S, nk2)
    kern = functools.partial(_swa_kernel, tq=tq, seq=S)
    return pl.pallas_call(
        kern,
        grid=(batch, nq),
        in_specs=[
            pl.BlockSpec(memory_space=pltpu.SMEM),
            pl.BlockSpec((tq, D), lambda bb, j: (bb * nq + j, 0)),
            pl.BlockSpec((D, D), lambda bb, j: (0, 0)),
            pl.BlockSpec((1, S, nk2), lambda bb, j: (bb, 0, 0)),
            pl.BlockSpec((1, S, nk2), lambda bb, j: (bb, 0, 0)),
            pl.BlockSpec((tq, 1), lambda bb, j: (bb * nq + j, 0)),
            pl.BlockSpec((3, LANES), lambda bb, j: (0, 0)),
            pl.BlockSpec((D, D), lambda bb, j: (0, 0)),
            pl.BlockSpec((1, D), lambda bb, j: (0, 0)),
            pl.BlockSpec((1, D), lambda bb, j: (0, 0)),
        ],
        out_specs=pl.BlockSpec((D, tq), lambda bb, j: (0, bb * nq + j)),
        out_shape=jax.ShapeDtypeStruct((D, T), F32),
        scratch_shapes=[pltpu.VMEM((tq, D), BF16), pltpu.VMEM((tq, D), BF16)],
        compiler_params=pltpu.CompilerParams(
            dimension_semantics=("parallel", "arbitrary"), vmem_limit_bytes=VMEM_LIMIT),
        name="swa_layer",
    )(sinks, h, w_q_bf16, k3, v3, pos, tab, w_o_bf16, g.reshape(1, D), b.reshape(1, D))


def kernel(x, positions, hg_w_in, hg_lb, hg_norm_w, hg_w_o, sw_w_q, sw_sinks, sw_w_o, w_kv,
           peer_w_q, peer_sub_keys, peer_u, peer_v, ln_mix_g, ln_mix_b, ln_ffn_g, ln_ffn_b):
    B, S, D = x.shape
    T = B * S
    H = HG_HEADS
    pos = positions.reshape(T, 1)
    tab = _rope_lane_tables()

    lb_all = jnp.cumsum(jax.nn.softmax(hg_lb.astype(F32), axis=0), axis=0)
    w_heads = (hg_w_in[0].reshape(D, 4, H, HG_DK).transpose(2, 0, 1, 3)
               .reshape(H, D, 4 * HG_DK).astype(BF16))
    mix = _hgrn2_mix(x.astype(BF16), w_heads, lb_all[0].reshape(H, 1, HG_DK),
                     hg_norm_w[0].reshape(H, 1, HG_DV))
    ht = _proj_res_ln(mix.reshape(T, H * HG_DV), hg_w_o[0].astype(BF16), x.reshape(T, D),
                      ln_mix_g[0], ln_mix_b[0])
    u_all = peer_u.astype(BF16)
    vt_all = jnp.swapaxes(peer_v, 1, 2).astype(BF16)
    h = _peer_layer(ht, 0, peer_w_q[0], peer_sub_keys[0], u_all, vt_all,
                    ln_ffn_g[0], ln_ffn_b[0])

    k_dup, v_dup = _shared_kv(h, w_kv, pos, tab)
    ht = _swa_layer(h, k_dup, v_dup, pos, tab, sw_w_q[0].astype(BF16), sw_sinks[0].astype(F32),
                    sw_w_o[0].astype(BF16), ln_mix_g[1], ln_mix_b[1], batch=B)
    h = _peer_layer(ht, 1, peer_w_q[1], peer_sub_keys[1], u_all, vt_all,
                    ln_ffn_g[1], ln_ffn_b[1])
    return h.reshape(B, S, D)
```

```python
import functools
import math

import jax
import jax.numpy as jnp
from jax import lax
from jax.experimental import pallas as pl
from jax.experimental.pallas import tpu as pltpu

F32 = jnp.float32
BF16 = jnp.bfloat16

D_MODEL = 1024
DEPTH = 2
HG_HEADS = 8
HG_DK = 128
HG_DV = 128
HG_CHUNK = 32
SW_Q_HEADS = 16
SW_KV_HEADS = 4
SW_HEAD_DIM = 64
SW_WINDOW = 128
ROPE_THETA = 500000.0
ROPE_DIMS = 16
PEER_HEADS = 8
PEER_NKEYS = 128
PEER_HALF = 128
PEER_TOPK = 16
DN_ALPHA = (2.0 * DEPTH) ** 0.25
LN_EPS = 1e-5
RMS_EPS = 1e-6

LANES = 128
VMEM_LIMIT = 56 * 1024 * 1024
NEG_INF = float("-inf")
INV_SQRT2 = 1.0 / math.sqrt(2.0)


def _sigmoid(x):
    return 1.0 / (1.0 + jnp.exp(-x))


def _nt_dot(a, b):
    return lax.dot_general(a, b, (((1,), (1,)), ((), ())), preferred_element_type=F32)


def _tn_dot(a, b):
    return lax.dot_general(a, b, (((0,), (0,)), ((), ())), preferred_element_type=F32)


def _hgrn2_kernel(x_ref, w_ref, lb_ref, nw_ref, o_ref, proj_ref, *, seq, chunk, row_tile):
    def project(i, carry):
        r = pl.ds(pl.multiple_of(i * row_tile, row_tile), row_tile)
        proj_ref[r, 0:4 * HG_DK] = jnp.dot(x_ref[0, r, :], w_ref[0], preferred_element_type=F32)
        return carry

    lax.fori_loop(0, seq // row_tile, project, 0)

    lb = lb_ref[0]
    nw = nw_ref[0]
    row = lax.broadcasted_iota(jnp.int32, (chunk, HG_DK), 0)
    causal = (lax.broadcasted_iota(jnp.int32, (chunk, chunk), 1)
              <= lax.broadcasted_iota(jnp.int32, (chunk, chunk), 0))

    def step(c, state_t):
        r = pl.ds(pl.multiple_of(c * chunk, chunk), chunk)
        q = proj_ref[r, 0:128]
        f = proj_ref[r, 128:256]
        v = proj_ref[r, 256:384]
        g = proj_ref[r, 384:512]
        q = q * _sigmoid(q)
        fg = lb + (1.0 - lb) * _sigmoid(f)
        k = 1.0 - fg
        b = jnp.log(fg)
        s = 1
        while s < chunk:
            b = b + jnp.where(row >= s, pltpu.roll(b, s, axis=0), 0.0)
            s *= 2
        b_last = b[chunk - 1:chunk, :]
        qd = (q * jnp.exp(b)).astype(BF16)
        kd = (k * jnp.exp(-b)).astype(BF16)
        kl = (k * jnp.exp(b_last - b)).astype(BF16)
        vb = v.astype(BF16)
        att = jnp.where(causal, _nt_dot(qd, kd), 0.0)
        o = (jnp.dot(att.astype(BF16), vb, preferred_element_type=F32)
             + _nt_dot(qd, state_t.astype(BF16)))
        state_t = state_t * jnp.exp(b_last) + _tn_dot(vb, kl)
        o = o * lax.rsqrt(jnp.mean(o * o, axis=-1, keepdims=True) + RMS_EPS)
        o_ref[0, r, :] = o * nw * (g * _sigmoid(g))
        return state_t

    lax.fori_loop(0, seq // chunk, step, jnp.zeros((HG_DV, HG_DK), F32), unroll=16)


def _hgrn2_mix(xb, w_heads, lb, norm_w):
    B, S, D = xb.shape
    H = w_heads.shape[0]
    kern = functools.partial(_hgrn2_kernel, seq=S, chunk=HG_CHUNK, row_tile=min(512, S))
    return pl.pallas_call(
        kern,
        grid=(B, H),
        in_specs=[
            pl.BlockSpec((1, S, D), lambda b, h: (b, 0, 0)),
            pl.BlockSpec((1, D, 4 * HG_DK), lambda b, h: (h, 0, 0)),
            pl.BlockSpec((1, 1, HG_DK), lambda b, h: (h, 0, 0)),
            pl.BlockSpec((1, 1, HG_DV), lambda b, h: (h, 0, 0)),
        ],
        out_specs=pl.BlockSpec((1, S, HG_DV), lambda b, h: (b, 0, h)),
        out_shape=jax.ShapeDtypeStruct((B, S, H * HG_DV), F32),
        scratch_shapes=[pltpu.VMEM((S, 4 * HG_DK + LANES), F32)],
        compiler_params=pltpu.CompilerParams(
            dimension_semantics=("parallel", "arbitrary"), vmem_limit_bytes=VMEM_LIMIT),
        name="hgrn2_mix",
    )(xb, w_heads, lb, norm_w)


def _layernorm_rows(z, g, b):
    mu = jnp.mean(z, axis=-1, keepdims=True)
    zc = z - mu
    var = jnp.mean(zc * zc, axis=-1, keepdims=True)
    return zc * lax.rsqrt(var + LN_EPS) * g + b


def _proj_res_ln_kernel(a_ref, w_ref, res_ref, g_ref, b_ref, o_ref):
    y = jnp.dot(a_ref[...].astype(BF16), w_ref[...], preferred_element_type=F32)
    o_ref[...] = _layernorm_rows(DN_ALPHA * res_ref[...] + y, g_ref[...], b_ref[...]).T


def _proj_res_ln(a, w_bf16, res, g, b, *, tm=512):
    T, K = a.shape
    N = w_bf16.shape[1]
    tm = min(tm, T)
    return pl.pallas_call(
        _proj_res_ln_kernel,
        grid=(T // tm,),
        in_specs=[
            pl.BlockSpec((tm, K), lambda i: (i, 0)),
            pl.BlockSpec((K, N), lambda i: (0, 0)),
            pl.BlockSpec((tm, N), lambda i: (i, 0)),
            pl.BlockSpec((1, N), lambda i: (0, 0)),
            pl.BlockSpec((1, N), lambda i: (0, 0)),
        ],
        out_specs=pl.BlockSpec((N, tm), lambda i: (0, i)),
        out_shape=jax.ShapeDtypeStruct((N, T), F32),
        compiler_params=pltpu.CompilerParams(
            dimension_semantics=("parallel",), vmem_limit_bytes=VMEM_LIMIT),
        name="proj_res_ln",
    )(a, w_bf16, res, g.reshape(1, N), b.reshape(1, N))


def _sorting_network(n):
    pairs = []

    def merge(lo, cnt, r):
        step = 2 * r
        if step < cnt:
            merge(lo, cnt, step)
            merge(lo + r, cnt, step)
            for i in range(lo + r, lo + cnt - r, step):
                pairs.append((i, i + r))
        else:
            pairs.append((lo, lo + r))

    def sort(lo, cnt):
        if cnt > 1:
            half = cnt // 2
            sort(lo, half)
            sort(lo + half, half)
            merge(lo, cnt, 1)

    sort(0, n)
    return pairs


_SORT16 = _sorting_network(PEER_TOPK)


def _pop_sorted_lists(lists, extra, n_pop):
    lists = list(lists)
    vals = []
    for t in range(n_pop):
        head = lists[0] if extra is None else jnp.maximum(lists[0], extra)
        m = jnp.max(head, axis=0, keepdims=True)
        vals.append(m)
        remaining = n_pop - 1 - t
        if remaining == 0:
            break
        hit = lists[0] == m
        for k in range(min(remaining, len(lists) - 1)):
            lists[k] = jnp.where(hit, lists[k + 1], lists[k])
        if remaining >= len(lists):
            lists[len(lists) - 1] = jnp.where(hit, NEG_INF, lists[len(lists) - 1])
        if extra is not None:
            extra = jnp.where(extra == m, NEG_INF, extra)
    return vals


def _top16_desc(s):
    v = [s[8 * k:8 * (k + 1)] for k in range(PEER_NKEYS // 8)]
    for i, j in _SORT16:
        v[i], v[j] = jnp.maximum(v[i], v[j]), jnp.minimum(v[i], v[j])
    return _pop_sorted_lists(v, None, PEER_TOPK)


def _rows_to_slab(vals, lo):
    n = vals[0].shape[1]
    rid = lax.broadcasted_iota(jnp.int32, (8, n), 0)
    slab = jnp.zeros((8, n), F32)
    for r in range(8):
        slab = jnp.where(rid == r, vals[lo + r], slab)
    return slab


def _peer_route_kernel(ht_ref, wq_ref, sk_ref, ga_ref, cnt_ref, gb_ref, rk_ref, q_scr, s_scr, *, tm):
    hb = ht_ref[...].astype(BF16)
    q_scr[...] = jnp.dot(wq_ref[...], hb, preferred_element_type=F32).astype(BF16)
    for hc in range(2 * PEER_HEADS):
        s_scr[hc, :, 0:tm] = jnp.dot(sk_ref[hc], q_scr[hc * PEER_HALF:(hc + 1) * PEER_HALF, :],
                                     preferred_element_type=F32)

    ncol = tm // LANES
    rid8 = lax.broadcasted_iota(jnp.int32, (8, LANES), 0)

    def body(it, carry):
        hh = it // ncol
        cc = it % ncol
        col = pl.ds(pl.multiple_of(cc * LANES, LANES), LANES)
        s1 = s_scr[2 * hh, :, col]
        s2 = s_scr[2 * hh + 1, :, col]
        a_vals = _top16_desc(s1)
        b_vals = _top16_desc(s2)
        rank2 = jnp.zeros(s2.shape, F32)
        for q in range(PEER_TOPK):
            rank2 = rank2 + jnp.where(s2 < b_vals[q], 1.0, 0.0)
        a_lo = _rows_to_slab(a_vals, 0)
        a_hi = _rows_to_slab(a_vals, 8)
        b_lo = _rows_to_slab(b_vals, 0)
        b_hi = _rows_to_slab(b_vals, 8)
        lists = []
        for p in range(PEER_TOPK):
            n_lists = sum(1 for q in range(8) if PEER_TOPK // (q + 1) > p)
            lists.append(jnp.where(rid8 < n_lists, a_vals[p] + b_lo, NEG_INF))
        thr = _pop_sorted_lists(lists, a_vals[0] + b_hi, PEER_TOPK)[-1]

        ea_lo = jnp.exp(a_lo - a_vals[0])
        ea_hi = jnp.exp(a_hi - a_vals[0])
        cnt = jnp.zeros(s1.shape, F32)
        zacc = jnp.zeros((8, LANES), F32)
        for q in range(PEER_TOPK):
            bq = b_vals[q]
            cnt = cnt + jnp.where(s1 + bq >= thr, 1.0, 0.0)
            part = (jnp.where(a_lo + bq >= thr, ea_lo, 0.0)
                    + jnp.where(a_hi + bq >= thr, ea_hi, 0.0))
            zacc = zacc + jnp.exp(bq - b_vals[0]) * part
        inv_z = 0.5 / jnp.sum(zacc, axis=0, keepdims=True)

        ga_ref[hh, cc] = jnp.exp(s1 - a_vals[0])
        cnt_ref[hh, cc] = cnt
        gb_ref[hh, cc] = (jnp.exp(s2 - b_vals[0]) * inv_z).astype(BF16)
        rk_ref[hh, cc] = rank2.astype(BF16)
        return carry

    lax.fori_loop(0, PEER_HEADS * ncol, body, 0)


def _peer_route(ht, wq_t, sub_keys, *, tm=512):
    D, T = ht.shape
    tm = min(tm, T)
    nq = wq_t.shape[0]
    kern = functools.partial(_peer_route_kernel, tm=tm)
    head_rows = pl.BlockSpec((PEER_HEADS, tm // LANES, PEER_NKEYS, LANES), lambda i: (0, i, 0, 0))
    shp = (PEER_HEADS, T // LANES, PEER_NKEYS, LANES)
    return pl.pallas_call(
        kern,
        grid=(T // tm,),
        in_specs=[
            pl.BlockSpec((D, tm), lambda i: (0, i)),
            pl.BlockSpec((nq, D), lambda i: (0, 0)),
            pl.BlockSpec((2 * PEER_HEADS, PEER_NKEYS, PEER_HALF), lambda i: (0, 0, 0)),
        ],
        out_specs=[head_rows, head_rows, head_rows, head_rows],
        out_shape=[jax.ShapeDtypeStruct(shp, F32), jax.ShapeDtypeStruct(shp, F32),
                   jax.ShapeDtypeStruct(shp, BF16), jax.ShapeDtypeStruct(shp, BF16)],
        scratch_shapes=[pltpu.VMEM((nq, tm), BF16),
                        pltpu.VMEM((2 * PEER_HEADS, PEER_NKEYS, tm + LANES), F32)],
        compiler_params=pltpu.CompilerParams(
            dimension_semantics=("parallel",), vmem_limit_bytes=VMEM_LIMIT),
        name="peer_route",
    )(ht, wq_t, sub_keys)


def _peer_expert_kernel(res_ref, u_ref, vt_ref, ga_ref, cnt_ref, gb_ref, rk_ref, g_ref,
                        b_ref, o_ref, xb_ref, a_scr, w_scr, rk16_scr, gb16_scr, acc_scr,
                        *, tm, ni):
    e = pl.program_id(1)

    @pl.when(e == 0)
    def _():
        acc_scr[:, 0:tm] = jnp.zeros((acc_scr.shape[0], tm), F32)
        xb_ref[...] = res_ref[...].astype(BF16)

        def repack_head(h, carry):
            for c in range(tm // LANES):
                rk16_scr[h, :, c * LANES:(c + 1) * LANES] = rk_ref[h, c]
                gb16_scr[h, :, c * LANES:(c + 1) * LANES] = gb_ref[h, c]
            return carry

        lax.fori_loop(0, PEER_HEADS, repack_head, 0)

    group = 4
    slab = 16
    nslab = PEER_NKEYS // slab
    per_iter = 4

    def routing_weight(a_rd, w_wr, it, sub):
        cc = it * per_iter + sub
        col = pl.ds(pl.multiple_of(it * (per_iter * LANES), per_iter * LANES) + sub * LANES, LANES)
        for g0 in range(0, ni, group):
            accs = [[None] * nslab for _ in range(group)]
            for h in range(PEER_HEADS):
                rk = rk16_scr[h, :, col]
                gb = gb16_scr[h, :, col]
                rks = [rk[i * slab:(i + 1) * slab] for i in range(nslab)]
                gbs = [gb[i * slab:(i + 1) * slab] for i in range(nslab)]
                for k in range(group):
                    il = g0 + k
                    n16 = jnp.broadcast_to(cnt_ref[h, cc, il:il + 1, :], (slab, LANES)).astype(BF16)
                    a16 = jnp.broadcast_to(ga_ref[h, cc, il:il + 1, :], (slab, LANES)).astype(BF16)
                    for i in range(nslab):
                        term = jnp.where(rks[i] < n16, gbs[i] * a16, jnp.zeros_like(a16))
                        accs[k][i] = term if h == 0 else accs[k][i] + term
            for k in range(group):
                r0 = (g0 + k) * PEER_NKEYS
                a = a_rd[r0:r0 + PEER_NKEYS, col].astype(BF16)
                two_gelu = a * (1.0 + lax.erf(a * INV_SQRT2))
                w_wr[r0:r0 + PEER_NKEYS, col] = jnp.concatenate(accs[k], axis=0) * two_gelu

    a_scr[:, 0:tm] = jnp.dot(u_ref[...], xb_ref[...], preferred_element_type=F32)

    def vpu_pass(it, carry):
        for sub in range(per_iter):
            routing_weight(a_scr, w_scr, it, sub)
        return carry

    lax.fori_loop(0, tm // LANES // per_iter, vpu_pass, 0)
    acc_scr[:, 0:tm] += jnp.dot(vt_ref[...], w_scr[:, 0:tm], preferred_element_type=F32)

    @pl.when(e == pl.num_programs(1) - 1)
    def _():
        d_model = acc_scr.shape[0]
        n_blk = d_model // 8
        inv_d = 1.0 / d_model

        def rows(i):
            return pl.ds(pl.multiple_of(i * 8, 8), 8)

        def add_z(i, acc):
            z = DN_ALPHA * res_ref[rows(i), :] + acc_scr[rows(i), 0:tm]
            acc_scr[rows(i), 0:tm] = z
            return acc + z

        zero = jnp.zeros((8, tm), F32)
        mu = jnp.sum(lax.fori_loop(0, n_blk, add_z, zero), axis=0, keepdims=True) * inv_d

        def add_sq(i, acc):
            zc = acc_scr[rows(i), 0:tm] - mu
            return acc + zc * zc

        var = jnp.sum(lax.fori_loop(0, n_blk, add_sq, zero), axis=0, keepdims=True) * inv_d
        rstd = lax.rsqrt(var + LN_EPS)
        for c in range(tm // LANES):
            col = slice(c * LANES, (c + 1) * LANES)
            y = (acc_scr[:, col] - mu[:, col]) * rstd[:, col] * g_ref[...] + b_ref[...]
            o_ref[col, :] = y.T


def _peer_experts(ht, u_all, vt_all, layer, ga, cnt, gb, rk, g, b, *, tm=1024, eb=1024):
    D, T = ht.shape
    E = u_all.shape[1]
    tm = min(tm, T)
    ni = eb // PEER_NKEYS
    kern = functools.partial(_peer_expert_kernel, tm=tm, ni=ni)
    once = pl.Buffered(1)
    return pl.pallas_call(
        kern,
        grid=(T // tm, E // eb),
        in_specs=[
            pl.BlockSpec((D, tm), lambda t, e: (0, t), pipeline_mode=once),
            pl.BlockSpec((None, eb, D), lambda t, e: (layer, e, 0)),
            pl.BlockSpec((None, D, eb), lambda t, e: (layer, 0, e)),
            pl.BlockSpec((PEER_HEADS, tm // LANES, ni, LANES), lambda t, e: (0, t, e, 0)),
            pl.BlockSpec((PEER_HEADS, tm // LANES, ni, LANES), lambda t, e: (0, t, e, 0)),
            pl.BlockSpec((PEER_HEADS, tm // LANES, PEER_NKEYS, LANES), lambda t, e: (0, t, 0, 0),
                         pipeline_mode=once),
            pl.BlockSpec((PEER_HEADS, tm // LANES, PEER_NKEYS, LANES), lambda t, e: (0, t, 0, 0),
                         pipeline_mode=once),
            pl.BlockSpec((D, 1), lambda t, e: (0, 0)),
            pl.BlockSpec((D, 1), lambda t, e: (0, 0)),
        ],
        out_specs=pl.BlockSpec((tm, D), lambda t, e: (t, 0)),
        out_shape=jax.ShapeDtypeStruct((T, D), F32),
        scratch_shapes=[pltpu.VMEM((D, tm), BF16),
                        pltpu.VMEM((eb, tm + LANES), F32), pltpu.VMEM((eb, tm + LANES), BF16),
                        pltpu.VMEM((PEER_HEADS, PEER_NKEYS, tm + LANES), BF16),
                        pltpu.VMEM((PEER_HEADS, PEER_NKEYS, tm + LANES), BF16),
                        pltpu.VMEM((D, tm + LANES), F32)],
        compiler_params=pltpu.CompilerParams(
            dimension_semantics=("parallel", "arbitrary"), vmem_limit_bytes=VMEM_LIMIT),
        name="peer_experts",
    )(ht, u_all, vt_all, ga, cnt, gb, rk, g.reshape(D, 1), b.reshape(D, 1))


def _peer_layer(ht, layer, w_pq, sub_keys, u_all, vt_all, g, b):
    wq_t = w_pq.T.astype(BF16)
    sk = sub_keys.reshape(2 * PEER_HEADS, PEER_NKEYS, PEER_HALF).astype(BF16)
    ga, cnt, gb, rk = _peer_route(ht, wq_t, sk)
    return _peer_experts(ht, u_all, vt_all, layer, ga, cnt, gb, rk, g, b)


def _rope_lane_tables():
    d = jnp.arange(LANES) % SW_HEAD_DIM
    half = ROPE_DIMS // 2
    inv = ROPE_THETA ** (-(2.0 * (d % half).astype(F32)) / ROPE_DIMS)
    inv = jnp.where(d < ROPE_DIMS, inv, 0.0)
    sgn_up = jnp.where(d < half, -1.0, 0.0)
    sgn_dn = jnp.where((d >= half) & (d < ROPE_DIMS), 1.0, 0.0)
    return jnp.stack([inv, sgn_up, sgn_dn]).astype(F32)


def _rope_apply(x, pos_f, tab):
    n, width = x.shape
    reps = width // LANES
    ang = pos_f * tab[0:1, :]
    cos = jnp.tile(jnp.cos(ang), (1, reps))
    sin = jnp.tile(jnp.sin(ang), (1, reps))
    s_up = jnp.tile(tab[1:2, :], (1, reps))
    s_dn = jnp.tile(tab[2:3, :], (1, reps))
    half = ROPE_DIMS // 2
    partner = pltpu.roll(x, width - half, axis=1) * s_up + pltpu.roll(x, half, axis=1) * s_dn
    return x * cos + partner * sin


def _kv_kernel(h_ref, w_ref, pos_ref, tab_ref, k_ref, v_ref):
    kv = jnp.dot(h_ref[...].astype(BF16), w_ref[...], preferred_element_type=F32)
    nk2 = SW_KV_HEADS * LANES
    k_ref[...] = _rope_apply(kv[:, :nk2], pos_ref[...].astype(F32), tab_ref[...]).astype(BF16)
    v_ref[...] = kv[:, nk2:].astype(BF16)


def _dup_heads(w):
    D, n = w.shape
    w3 = w.reshape(D, n // SW_HEAD_DIM, SW_HEAD_DIM)
    return jnp.concatenate([w3, w3], axis=2).reshape(D, 2 * n)


def _shared_kv(h, w_kv, pos, tab, *, tm=512):
    T, D = h.shape
    tm = min(tm, T)
    nk = SW_KV_HEADS * SW_HEAD_DIM
    nk2 = SW_KV_HEADS * LANES
    w_kv_bf16 = jnp.concatenate([_dup_heads(w_kv[:, :nk]), _dup_heads(w_kv[:, nk:])],
                                axis=1).astype(BF16)
    return pl.pallas_call(
        _kv_kernel,
        grid=(T // tm,),
        in_specs=[
            pl.BlockSpec((tm, D), lambda i: (i, 0)),
            pl.BlockSpec(w_kv_bf16.shape, lambda i: (0, 0)),
            pl.BlockSpec((tm, 1), lambda i: (i, 0)),
            pl.BlockSpec((3, LANES), lambda i: (0, 0)),
        ],
        out_specs=[pl.BlockSpec((tm, nk2), lambda i: (i, 0)),
                   pl.BlockSpec((tm, nk2), lambda i: (i, 0))],
        out_shape=[jax.ShapeDtypeStruct((T, nk2), BF16), jax.ShapeDtypeStruct((T, nk2), BF16)],
        compiler_params=pltpu.CompilerParams(
            dimension_semantics=("parallel",), vmem_limit_bytes=VMEM_LIMIT),
        name="shared_kv",
    )(h, w_kv_bf16, pos, tab)


def _swa_kernel(sink_ref, h_ref, wq_ref, k_ref, v_ref, pos_ref, tab_ref, wo_ref, g_ref, b_ref,
                o_ref, q_scr, att_scr, *, tq):
    j = pl.program_id(1)
    W = SW_WINDOW
    hq = h_ref[...]
    q = jnp.dot(hq.astype(BF16), wq_ref[...], preferred_element_type=F32)
    q = _rope_apply(q, pos_ref[...].astype(F32), tab_ref[...]) * (SW_HEAD_DIM ** -0.5)
    q_scr[...] = q.astype(BF16)

    qi = lax.broadcasted_iota(jnp.int32, (W, 2 * W), 0)
    kj = lax.broadcasted_iota(jnp.int32, (W, 2 * W), 1)
    lane = lax.broadcasted_iota(jnp.int32, (W, LANES), 1)
    first_half = lane < SW_HEAD_DIM
    group = SW_Q_HEADS // SW_KV_HEADS

    def block(n, carry):
        q0 = j * tq + n * W
        k0 = jnp.maximum(q0 - W, 0)
        rows = pl.ds(pl.multiple_of(n * W, W), W)
        band = pl.ds(pl.multiple_of(k0, W), 2 * W)
        diff = (q0 + qi) - (k0 + kj)
        valid = (diff >= 0) & (diff < W)
        for pair in range(SW_Q_HEADS // 2):
            kvh = (2 * pair) // group
            lanes = slice(pair * LANES, (pair + 1) * LANES)
            qp = q_scr[rows, lanes]
            kb = k_ref[0, band, kvh * LANES:(kvh + 1) * LANES]
            vb = v_ref[0, band, kvh * LANES:(kvh + 1) * LANES]
            outs = []
            for sub in range(2):
                head = 2 * pair + sub
                keep = first_half if sub == 0 else jnp.logical_not(first_half)
                qh = jnp.where(keep, qp, jnp.zeros_like(qp))
                logits = jnp.where(valid, _nt_dot(qh, kb), NEG_INF)
                sink = sink_ref[head]
                m = jnp.maximum(jnp.max(logits, axis=-1, keepdims=True), sink)
                p = jnp.exp(logits - m)
                denom = jnp.sum(p, axis=-1, keepdims=True) + jnp.exp(sink - m)
                p = p * (1.0 / denom)
                outs.append(jnp.dot(p.astype(BF16), vb, preferred_element_type=F32))
            att_scr[rows, lanes] = jnp.where(first_half, outs[0], outs[1]).astype(BF16)
        return carry

    lax.fori_loop(0, tq // W, block, 0)
    y = jnp.dot(att_scr[...], wo_ref[...], preferred_element_type=F32)
    o_ref[...] = _layernorm_rows(DN_ALPHA * hq + y, g_ref[...], b_ref[...]).T


def _swa_layer(h, k_dup, v_dup, pos, tab, w_q_bf16, sinks, w_o_bf16, g, b, *, batch, tq=512):
    T, D = h.shape
    S = T // batch
    tq = min(tq, S)
    nq = S // tq
    nk2 = k_dup.shape[1]
    k3 = k_dup.reshape(batch, S, nk2)
    v3 = v_dup.reshape(batch, S, nk2)
    kern = functools.partial(_swa_kernel, tq=tq)
    return pl.pallas_call(
        kern,
        grid=(batch, nq),
        in_specs=[
            pl.BlockSpec(memory_space=pltpu.SMEM),
            pl.BlockSpec((tq, D), lambda bb, j: (bb * nq + j, 0)),
            pl.BlockSpec((D, D), lambda bb, j: (0, 0)),
            pl.BlockSpec((1, S, nk2), lambda bb, j: (bb, 0, 0)),
            pl.BlockSpec((1, S, nk2), lambda bb, j: (bb, 0, 0)),
            pl.BlockSpec((tq, 1), lambda bb, j: (bb * nq + j, 0)),
            pl.BlockSpec((3, LANES), lambda bb, j: (0, 0)),
            pl.BlockSpec((D, D), lambda bb, j: (0, 0)),
            pl.BlockSpec((1, D), lambda bb, j: (0, 0)),
            pl.BlockSpec((1, D), lambda bb, j: (0, 0)),
        ],
        out_specs=pl.BlockSpec((D, tq), lambda bb, j: (0, bb * nq + j)),
        out_shape=jax.ShapeDtypeStruct((D, T), F32),
        scratch_shapes=[pltpu.VMEM((tq, D), BF16), pltpu.VMEM((tq, D), BF16)],
        compiler_params=pltpu.CompilerParams(
            dimension_semantics=("parallel", "arbitrary"), vmem_limit_bytes=VMEM_LIMIT),
        name="swa_layer",
    )(sinks, h, w_q_bf16, k3, v3, pos, tab, w_o_bf16, g.reshape(1, D), b.reshape(1, D))


def kernel(x, positions, hg_w_in, hg_lb, hg_norm_w, hg_w_o, sw_w_q, sw_sinks, sw_w_o, w_kv,
           peer_w_q, peer_sub_keys, peer_u, peer_v, ln_mix_g, ln_mix_b, ln_ffn_g, ln_ffn_b):
    B, S, D = x.shape
    T = B * S
    H = HG_HEADS
    pos = positions.reshape(T, 1)
    tab = _rope_lane_tables()

    lb_all = jnp.cumsum(jax.nn.softmax(hg_lb.astype(F32), axis=0), axis=0)
    w_heads = (hg_w_in[0].reshape(D, 4, H, HG_DK).transpose(2, 0, 1, 3)
               .reshape(H, D, 4 * HG_DK).astype(BF16))
    mix = _hgrn2_mix(x.astype(BF16), w_heads, lb_all[0].reshape(H, 1, HG_DK),
                     hg_norm_w[0].reshape(H, 1, HG_DV))
    ht = _proj_res_ln(mix.reshape(T, H * HG_DV), hg_w_o[0].astype(BF16), x.reshape(T, D),
                      ln_mix_g[0], ln_mix_b[0])
    u_all = peer_u.astype(BF16)
    vt_all = jnp.swapaxes(peer_v, 1, 2).astype(BF16)
    h = _peer_layer(ht, 0, peer_w_q[0], peer_sub_keys[0], u_all, vt_all,
                    ln_ffn_g[0], ln_ffn_b[0])

    k_dup, v_dup = _shared_kv(h, w_kv, pos, tab)
    ht = _swa_layer(h, k_dup, v_dup, pos, tab, sw_w_q[0].astype(BF16), sw_sinks[0].astype(F32),
                    sw_w_o[0].astype(BF16), ln_mix_g[1], ln_mix_b[1], batch=B)
    h = _peer_layer(ht, 1, peer_w_q[1], peer_sub_keys[1], u_all, vt_all,
                    ln_ffn_g[1], ln_ffn_b[1])
    return h.reshape(B, S, D)
```
